```python
import math
import jax, jax.numpy as jnp
from jax import lax
import numpy as np

D_MODEL = 1024
BATCH = 8
SEQ = 2048
DEPTH = 1
DEC_BATCH = 128
DEC_SEQ = 1
PAST_LEN = 16384
PAGE_SIZE = 128

D_MIX = D_MODEL
HG_WIDTH = D_MIX // 2
GDN_WIDTH = D_MIX - HG_WIDTH
HG_HEADS = 4
HG_KEY = 128
HG_VAL = HG_WIDTH // HG_HEADS
GDN_HEADS = 4
GDN_DK = 128
GDN_DV = GDN_WIDTH // GDN_HEADS
CONV_W = 4
CHUNK = 64
EPS = 1e-6

SPLITS = (HG_HEADS * HG_KEY, HG_HEADS * HG_KEY, HG_WIDTH, HG_WIDTH,
          GDN_HEADS * GDN_DK, GDN_HEADS * GDN_DK, GDN_WIDTH, GDN_WIDTH,
          GDN_HEADS, GDN_HEADS)
D_IN = sum(SPLITS)
SPLIT_POINTS = tuple(int(v) for v in np.cumsum(SPLITS)[:-1])
CONV_CH = 2 * GDN_HEADS * GDN_DK + GDN_WIDTH

kernel_name = "hgrn2_gated_deltanet_parallel_heads_step"


def rmsnorm(x, w):
    x32 = x.astype(jnp.float32)
    y = x32 * lax.rsqrt(jnp.mean(x32 * x32, axis=-1, keepdims=True) + EPS)
    return (y * w.astype(jnp.float32)).astype(x.dtype)


def l2norm(x):
    return x * lax.rsqrt(jnp.sum(x * x, axis=-1, keepdims=True) + EPS)


def _chunks(a):
    B, T = a.shape[:2]
    a = a.reshape((B, T // CHUNK, CHUNK) + a.shape[2:])
    return jnp.moveaxis(jnp.moveaxis(a, 1, 0), 2, 3)


def _unchunks(o):
    n, B, H, C, V = o.shape
    return jnp.moveaxis(jnp.moveaxis(o, 0, 1), 3, 2).reshape(B, n * C, H, V)


def hgrn2_chunked(q, k, v, log_f, s0):
    causal = jnp.tril(jnp.ones((CHUNK, CHUNK), dtype=bool))

    def step(s, inp):
        qc, kc, vc, lfc = inp
        b = jnp.cumsum(lfc, axis=2)
        diff = b[:, :, :, None, :] - b[:, :, None, :, :]
        decay = jnp.exp(jnp.where(causal[:, :, None], diff, -jnp.inf))
        scores = jnp.einsum('bhtk,bhtsk,bhsk->bhts', qc, decay, kc)
        o = (jnp.einsum('bhts,bhsv->bhtv', scores, vc)
             + jnp.einsum('bhtk,bhkv->bhtv', qc * jnp.exp(b), s))
        b_last = b[:, :, -1:, :]
        s_new = (jnp.exp(b_last[:, :, 0, :])[..., None] * s
                 + jnp.einsum('bhsk,bhsv->bhkv', kc * jnp.exp(b_last - b), vc))
        return s_new, o

    s_fin, o = lax.scan(step, s0, (_chunks(q), _chunks(k), _chunks(v), _chunks(log_f)))
    return _unchunks(o), s_fin


def hgrn2_recurrent(q, k, v, log_f, s0):
    def step(s, inp):
        qt, kt, vt, lft = inp
        s = jnp.exp(lft)[..., None] * s + kt[..., None] * vt[:, :, None, :]
        return s, jnp.einsum('bhk,bhkv->bhv', qt, s)

    tm = lambda a: jnp.moveaxis(a, 1, 0)
    s_fin, o = lax.scan(step, s0, (tm(q), tm(k), tm(v), tm(log_f)))
    return jnp.moveaxis(o, 0, 1), s_fin


def gdn_chunked(q, k, v, beta, g, s0):
    causal = jnp.tril(jnp.ones((CHUNK, CHUNK), dtype=bool))
    strict = jnp.tril(jnp.ones((CHUNK, CHUNK), dtype=bool), k=-1)
    eye = jnp.eye(CHUNK, dtype=jnp.float32)
    V = v.shape[-1]

    def step(s, inp):
        qc, kc, vc, bc, gc = inp
        G = jnp.cumsum(gc, axis=-1)
        diff = G[..., :, None] - G[..., None, :]
        L = jnp.exp(jnp.where(causal, diff, -jnp.inf))
        kb = kc * bc[..., None]
        A = jnp.where(strict, jnp.einsum('bhtk,bhsk->bhts', kb, kc) * L, 0.0) + eye
        rhs = jnp.concatenate([vc * bc[..., None], kb * jnp.exp(G)[..., None]], axis=-1)
        sol = lax.linalg.triangular_solve(A, rhs, left_side=True, lower=True,
                                          unit_diagonal=True)
        u = sol[..., :V] - jnp.einsum('bhtk,bhkv->bhtv', sol[..., V:], s)
        attn = jnp.einsum('bhtk,bhsk->bhts', qc, kc) * L
        o = (jnp.einsum('bhtk,bhkv->bhtv', qc * jnp.exp(G)[..., None], s)
             + jnp.einsum('bhts,bhsv->bhtv', attn, u))
        G_last = G[..., -1:]
        s_new = (jnp.exp(G_last)[..., None] * s
                 + jnp.einsum('bhsk,bhsv->bhkv', kc * jnp.exp(G_last - G)[..., None], u))
        return s_new, o

    s_fin, o = lax.scan(step, s0, (_chunks(q), _chunks(k), _chunks(v),
                                   _chunks(beta), _chunks(g)))
    return _unchunks(o), s_fin


def gdn_recurrent(q, k, v, beta, g, s0):
    def step(s, inp):
        qt, kt, vt, bt, gt = inp
        s = jnp.exp(gt)[..., None, None] * s
        delta = (vt - jnp.einsum('bhk,bhkv->bhv', kt, s)) * bt[..., None]
        s = s + kt[..., None] * delta[:, :, None, :]
        return s, jnp.einsum('bhk,bhkv->bhv', qt, s)

    tm = lambda a: jnp.moveaxis(a, 1, 0)
    s_fin, o = lax.scan(step, s0, (tm(q), tm(k), tm(v), tm(beta), tm(g)))
    return jnp.moveaxis(o, 0, 1), s_fin


def hybrid_mixer(h, w_in, conv_w, lb, a_log, dt_bias, hg_norm, gdn_norm, w_out,
                 s_hg, s_gdn, conv_prev, chunked):
    f32 = jnp.float32
    B, T, _ = h.shape
    p = jnp.einsum('btd,de->bte', h, w_in).astype(f32)
    hq, hf, hi, hz, gq, gk, gv, gz, gb, ga = jnp.split(p, SPLIT_POINTS, axis=-1)

    fg = lb + (1.0 - lb) * jax.nn.sigmoid(hf)
    hg_q = hq.reshape(B, T, HG_HEADS, HG_KEY)
    hg_k = (1.0 - fg).reshape(B, T, HG_HEADS, HG_KEY)
    hg_lf = jnp.log(fg).reshape(B, T, HG_HEADS, HG_KEY)
    hg_v = hi.reshape(B, T, HG_HEADS, HG_VAL)

    qkv = jnp.concatenate([gq, gk, gv], axis=-1)
    padded = jnp.concatenate([conv_prev.astype(f32), qkv], axis=1)
    cw = conv_w.astype(f32)
    conv = sum(padded[:, j:j + T] * cw[j] for j in range(CONV_W))
    qkv_c = jax.nn.silu(conv)
    new_conv = padded[:, T:]
    cq, ck, cv = jnp.split(qkv_c, (GDN_HEADS * GDN_DK, 2 * GDN_HEADS * GDN_DK), axis=-1)
    gdn_q = l2norm(cq.reshape(B, T, GDN_HEADS, GDN_DK)) * (GDN_DK ** -0.5)
    gdn_k = l2norm(ck.reshape(B, T, GDN_HEADS, GDN_DK))
    gdn_v = cv.reshape(B, T, GDN_HEADS, GDN_DV)
    beta = jax.nn.sigmoid(gb)
    g = -jnp.exp(a_log.astype(f32)) * jax.nn.softplus(ga + dt_bias.astype(f32))

    s_hg = s_hg.astype(f32)
    s_gdn = s_gdn.astype(f32)
    if chunked:
        o_hg, s_hg_new = hgrn2_chunked(hg_q, hg_k, hg_v, hg_lf, s_hg)
        o_gdn, s_gdn_new = gdn_chunked(gdn_q, gdn_k, gdn_v, beta, g, s_gdn)
    else:
        o_hg, s_hg_new = hgrn2_recurrent(hg_q, hg_k, hg_v, hg_lf, s_hg)
        o_gdn, s_gdn_new = gdn_recurrent(gdn_q, gdn_k, gdn_v, beta, g, s_gdn)

    o_hg = rmsnorm(o_hg, hg_norm) * jax.nn.silu(hz.reshape(B, T, HG_HEADS, HG_VAL))
    o_gdn = rmsnorm(o_gdn, gdn_norm) * jax.nn.silu(gz.reshape(B, T, GDN_HEADS, GDN_DV))
    o = jnp.concatenate([o_hg.reshape(B, T, HG_WIDTH), o_gdn.reshape(B, T, GDN_WIDTH)], axis=-1)
    out = jnp.einsum('bte,ed->btd', o.astype(h.dtype), w_out)
    return out.astype(h.dtype), s_hg_new, s_gdn_new, new_conv


def setup_inputs(seed: int = 0) -> dict:
    key = jax.random.key(seed)
    ks = jax.random.split(key, 16)
    f32 = jnp.float32
    x_prompt = jax.random.normal(ks[0], (BATCH, SEQ, D_MODEL), f32)
    x_sample = jax.random.normal(ks[1], (DEC_BATCH, DEC_SEQ, D_MODEL), f32)
    state_hgrn = jax.random.normal(ks[2], (DEPTH, DEC_BATCH, HG_HEADS, HG_KEY, HG_VAL), f32) * 0.5
    state_gdn = jax.random.normal(ks[3], (DEPTH, DEC_BATCH, GDN_HEADS, GDN_DK, GDN_DV), f32) * (GDN_DK ** -0.5)
    state_gdn_conv = jax.random.normal(ks[4], (DEPTH, DEC_BATCH, CONV_W - 1, CONV_CH), f32)
    norm_w = 1.0 + 0.01 * jax.random.normal(ks[5], (DEPTH, D_MODEL), f32)
    w_in = jax.random.normal(ks[6], (DEPTH, D_MODEL, D_IN), f32) * (D_MODEL ** -0.5)
    hg_lb_logits = 0.1 * jax.random.normal(ks[7], (DEPTH + 1, HG_HEADS * HG_KEY), f32)
    conv_w = jax.random.normal(ks[8], (DEPTH, CONV_W, CONV_CH), f32) * (CONV_W ** -0.5)
    gdn_a_log = jnp.log(jax.random.uniform(ks[9], (DEPTH, GDN_HEADS), f32, 1.0, 16.0))
    dt = jnp.exp(jax.random.uniform(ks[10], (DEPTH, GDN_HEADS), f32, math.log(1e-3), math.log(1e-1)))
    gdn_dt_bias = dt + jnp.log(-jnp.expm1(-dt))
    hg_out_norm = 1.0 + 0.01 * jax.random.normal(ks[11], (DEPTH, HG_VAL), f32)
    gdn_out_norm = 1.0 + 0.01 * jax.random.normal(ks[12], (DEPTH, GDN_DV), f32)
    w_out = jax.random.normal(ks[13], (DEPTH, D_MIX, D_MODEL), f32) * (D_MIX ** -0.5)
    final_norm = 1.0 + 0.01 * jax.random.normal(ks[14], (D_MODEL,), f32)
    return {"x_prompt": x_prompt, "x_sample": x_sample,
            "state_hgrn": state_hgrn, "state_gdn": state_gdn, "state_gdn_conv": state_gdn_conv,
            "norm_w": norm_w, "w_in": w_in, "hg_lb_logits": hg_lb_logits, "conv_w": conv_w,
            "gdn_a_log": gdn_a_log, "gdn_dt_bias": gdn_dt_bias,
            "hg_out_norm": hg_out_norm, "gdn_out_norm": gdn_out_norm,
            "w_out": w_out, "final_norm": final_norm}


def reference(x_prompt, x_sample, state_hgrn, state_gdn, state_gdn_conv,
              norm_w, w_in, hg_lb_logits, conv_w, gdn_a_log, gdn_dt_bias,
              hg_out_norm, gdn_out_norm, w_out, final_norm):
    f32 = jnp.float32
    lower_bounds = jnp.cumsum(jax.nn.softmax(hg_lb_logits.astype(f32), axis=0), axis=0)
    hp, hs = x_prompt, x_sample
    p_hg, p_gdn, p_conv, s_hg_l, s_gdn_l, s_conv_l = [], [], [], [], [], []
    for l in range(DEPTH):
        lb = lower_bounds[l]
        z_hg = jnp.zeros((BATCH, HG_HEADS, HG_KEY, HG_VAL), f32)
        z_gdn = jnp.zeros((BATCH, GDN_HEADS, GDN_DK, GDN_DV), f32)
        z_conv = jnp.zeros((BATCH, CONV_W - 1, CONV_CH), f32)
        dp, a, b, c = hybrid_mixer(rmsnorm(hp, norm_w[l]), w_in[l], conv_w[l], lb,
                                   gdn_a_log[l], gdn_dt_bias[l], hg_out_norm[l],
                                   gdn_out_norm[l], w_out[l], z_hg, z_gdn, z_conv, True)
        hp = hp + dp
        p_hg.append(a.astype(x_prompt.dtype))
        p_gdn.append(b.astype(x_prompt.dtype))
        p_conv.append(c.astype(x_prompt.dtype))
        ds, a, b, c = hybrid_mixer(rmsnorm(hs, norm_w[l]), w_in[l], conv_w[l], lb,
                                   gdn_a_log[l], gdn_dt_bias[l], hg_out_norm[l],
                                   gdn_out_norm[l], w_out[l], state_hgrn[l], state_gdn[l],
                                   state_gdn_conv[l], False)
        hs = hs + ds
        s_hg_l.append(a.astype(state_hgrn.dtype))
        s_gdn_l.append(b.astype(state_gdn.dtype))
        s_conv_l.append(c.astype(state_gdn_conv.dtype))
    y_prompt = rmsnorm(hp, final_norm)
    y_sample = rmsnorm(hs, final_norm)
    new_hgrn_prompt = jnp.stack(p_hg, axis=0)
    new_gdn_prompt = jnp.stack(p_gdn, axis=0)
    new_conv_prompt = jnp.stack(p_conv, axis=0)
    new_hgrn_sample = jnp.stack(s_hg_l, axis=0)
    new_gdn_sample = jnp.stack(s_gdn_l, axis=0)
    new_conv_sample = jnp.stack(s_conv_l, axis=0)
    return (y_prompt, y_sample, new_hgrn_prompt, new_gdn_prompt, new_conv_prompt,
            new_hgrn_sample, new_gdn_sample, new_conv_sample)
```

```python
import functools

import jax
import jax.numpy as jnp
from jax import lax
from jax.experimental import pallas as pl
from jax.experimental.pallas import tpu as pltpu

f32 = jnp.float32
bf16 = jnp.bfloat16

HEADS = 4
HD = 128
GW = HEADS * HD
N_GROUPS = 8
CONV_W = 4
CHUNK = 64
SUB = 8
EPS = 1e-6
TOK_BLOCK = 512
DEC_BLOCK = 8
VMEM_LIMIT = 56 * 1024 * 1024


def _mm(a, b):
    return jnp.dot(a.astype(bf16), b.astype(bf16), preferred_element_type=f32)


def _mm_nt(a, b):
    return lax.dot_general(a.astype(bf16), b.astype(bf16), (((1,), (1,)), ((), ())),
                           preferred_element_type=f32)


def _mm_tn(a, b):
    return lax.dot_general(a.astype(bf16), b.astype(bf16), (((0,), (0,)), ((), ())),
                           preferred_element_type=f32)


def _split3(x):
    a1 = x.astype(bf16)
    r1 = x - a1.astype(f32)
    a2 = r1.astype(bf16)
    r2 = r1 - a2.astype(f32)
    return a1, a2, r2.astype(bf16)


def _cumsum_rows(tril, x):
    a1, a2, a3 = _split3(x)
    d = lambda a: jnp.dot(tril, a, preferred_element_type=f32)
    return d(a1) + d(a2) + d(a3)


def _cumsum_lanes(x, triu):
    a1, a2, a3 = _split3(x)
    d = lambda a: jnp.dot(a, triu, preferred_element_type=f32)
    return d(a1) + d(a2) + d(a3)


def _rms(x, w):
    return x * lax.rsqrt(jnp.mean(x * x, axis=-1, keepdims=True) + EPS) * w


def _l2n(x):
    return x * lax.rsqrt(jnp.sum(x * x, axis=-1, keepdims=True) + EPS)


def _lower_bound(lbl):
    m = jnp.max(lbl, axis=0, keepdims=True)
    e = jnp.exp(lbl - m)
    return e[0:1, :] / jnp.sum(e, axis=0, keepdims=True)


def _chunk_masks():
    t = lax.broadcasted_iota(jnp.int32, (CHUNK, CHUNK), 0)
    s = lax.broadcasted_iota(jnp.int32, (CHUNK, CHUNK), 1)
    m = {}
    m["causal"] = s <= t
    m["strict"] = s < t
    m["eye"] = jnp.where(s == t, 1.0, 0.0).astype(f32)
    m["tril"] = jnp.where(s <= t, 1.0, 0.0).astype(bf16)
    m["triu"] = jnp.where(t <= s, 1.0, 0.0).astype(bf16)
    m["blk"] = (t // SUB) == (s // SUB)
    m["band"] = [(s == t - d) & ((t % SUB) >= d) for d in range(SUB)]
    lv = {}
    w = SUB
    while w < CHUNK:
        lv[w] = ((t // (2 * w)) == (s // (2 * w))) & (((t // w) % 2) == 1) & (((s // w) % 2) == 0)
        w *= 2
    m["lvl"] = lv
    return m


def _tri_inv(n, m):
    eye = m["eye"]
    nb = jnp.where(m["blk"], -n, 0.0)
    t = eye + nb
    p = nb
    w = 2
    while w < SUB:
        p = _mm(p, p)
        t = _mm(t, eye + p)
        w *= 2
    w = SUB
    while w < CHUNK:
        f = jnp.where(m["lvl"][w], n, 0.0)
        t = t - _mm(t, _mm(f, t))
        w *= 2
    return t


def _hgrn_chunk(q, hf, v, lbv, st_ref, m):
    fg = lbv + (1.0 - lbv) * jax.nn.sigmoid(hf)
    kk = 1.0 - fg
    b = _cumsum_rows(m["tril"], jnp.log(fg))
    nsub = CHUNK // SUB
    b3 = b.reshape(nsub, SUB, GW)
    q3 = q.reshape(nsub, SUB, GW)
    k3 = kk.reshape(nsub, SUB, GW)
    last = [b3[j, SUB - 1:SUB, :] for j in range(nsub)]
    blast = last[nsub - 1]
    zero = jnp.zeros((SUB, GW), f32)

    lvl_q, lvl_k = {}, {}
    w = SUB
    while w < CHUNK:
        per = w // SUB
        qs, ks = [], []
        for j in range(nsub):
            blk = j // per
            if blk % 2 == 1:
                qs.append(q3[j] * jnp.exp(b3[j] - last[blk * per - 1]))
                ks.append(zero)
            else:
                qs.append(zero)
                ks.append(k3[j] * jnp.exp(last[blk * per + per - 1] - b3[j]))
        lvl_q[w] = jnp.concatenate(qs, axis=0)
        lvl_k[w] = jnp.concatenate(ks, axis=0)
        w *= 2

    band = []
    for d in range(SUB):
        if d == 0:
            band.append(q * kk)
        else:
            band.append(q * pltpu.roll(kk, d, 0) * jnp.exp(b - pltpu.roll(b, d, 0)))

    qb = q * jnp.exp(b)
    kb = kk * jnp.exp(blast - b)
    eb = jnp.exp(blast)
    outs = []
    for h in range(HEADS):
        hs = slice(h * HD, (h + 1) * HD)
        sc = jnp.zeros((CHUNK, CHUNK), f32)
        for d in range(SUB):
            wd = jnp.sum(band[d][:, hs], axis=1, keepdims=True)
            sc = sc + jnp.where(m["band"][d], wd, 0.0)
        for w in lvl_q:
            sc = sc + jnp.where(m["lvl"][w], _mm_nt(lvl_q[w][:, hs], lvl_k[w][:, hs]), 0.0)
        st = st_ref[h]
        outs.append(_mm(sc, v[:, hs]) + _mm_nt(qb[:, hs], st))
        st_ref[h] = st * eb[:, hs] + _mm_tn(v[:, hs], kb[:, hs])
    return jnp.concatenate(outs, axis=1)


def _gdn_chunk(cq, ck, cv, ps, pst, arow, dtrow, acol, dtcol, s_ref, m):
    beta = jax.nn.sigmoid(ps)
    gcol = _cumsum_rows(m["tril"], -jnp.exp(arow) * jax.nn.softplus(ps + dtrow))
    grow = _cumsum_lanes(-jnp.exp(acol) * jax.nn.softplus(pst + dtcol), m["triu"])
    outs = []
    for h in range(HEADS):
        hs = slice(h * HD, (h + 1) * HD)
        qn = _l2n(cq[:, hs]) * (HD ** -0.5)
        kn = _l2n(ck[:, hs])
        vh = cv[:, hs]
        bc = beta[:, h:h + 1]
        gc = gcol[:, HEADS + h:HEADS + h + 1]
        gr = grow[HEADS + h:HEADS + h + 1, :]
        lmat = jnp.where(m["causal"], jnp.exp(gc - gr), 0.0)
        kbeta = kn * bc
        n = jnp.where(m["strict"], _mm_nt(kbeta, kn) * lmat, 0.0)
        tinv = _tri_inv(n, m)
        eg = jnp.exp(gc)
        sol = _mm(tinv, jnp.concatenate([vh * bc, kbeta * eg], axis=1))
        s = s_ref[h]
        u = sol[:, :HD] - _mm(sol[:, HD:], s)
        attn = jnp.where(m["causal"], _mm_nt(qn, kn) * lmat, 0.0)
        outs.append(_mm(qn * eg, s) + _mm(attn, u))
        glast = gc[CHUNK - 1:CHUNK, :]
        s_ref[h] = jnp.exp(glast) * s + _mm_tn(kn * jnp.exp(glast - gc), u)
    return jnp.concatenate(outs, axis=1)


def _gate_norm(o, gate, w):
    parts = []
    for h in range(HEADS):
        hs = slice(h * HD, (h + 1) * HD)
        parts.append(_rms(o[:, hs], w) * jax.nn.silu(gate[:, hs]))
    return jnp.concatenate(parts, axis=1)


def _prompt_body(x_ref, normw_ref, wmain_ref, wsmall_ref, wsmallt_ref, lbl_ref, convw_ref,
                 arow_ref, dtrow_ref, acol_ref, dtcol_ref, hgn_ref, gdnn_ref, fnorm_ref, wout_ref,
                 y_ref, shg_ref, sgdn_ref, sconv_ref,
                 p_ref, ps_ref, hbf_ref, obuf_ref, st_ref, sg_ref):
    ti = pl.program_id(1)
    tb = x_ref.shape[0]

    @pl.when(ti == 0)
    def _():
        st_ref[...] = jnp.zeros_like(st_ref)
        sg_ref[...] = jnp.zeros_like(sg_ref)
        p_ref[4:7, 0:SUB, :] = jnp.zeros((3, SUB, GW), f32)

    x = x_ref[...]
    hb = _rms(x, normw_ref[...]).astype(bf16)
    hbf_ref[...] = hb
    for g in range(N_GROUPS):
        p_ref[g, SUB:SUB + tb, :] = jnp.dot(hb, wmain_ref[g], preferred_element_type=f32)
    ps_ref[...] = jnp.dot(hb, wsmall_ref[...], preferred_element_type=f32)

    m = _chunk_masks()
    lbv = _lower_bound(lbl_ref[...])
    cw = convw_ref[...]
    arow, dtrow, acol, dtcol = arow_ref[...], dtrow_ref[...], acol_ref[...], dtcol_ref[...]
    hgn, gdnn = hgn_ref[...], gdnn_ref[...]
    wsmallt = wsmallt_ref[...]

    def chunk(c, carry):
        r0 = pl.multiple_of(c * CHUNK, CHUNK)
        rows = pl.ds(r0 + SUB, CHUNK)
        o_hg = _hgrn_chunk(p_ref[0, rows, :], p_ref[1, rows, :], p_ref[2, rows, :], lbv, st_ref, m)
        obuf_ref[pl.ds(r0, CHUNK), 0:GW] = _gate_norm(o_hg, p_ref[3, rows, :], hgn).astype(bf16)

        conv = []
        for i in range(3):
            win = p_ref[4 + i, pl.ds(r0, CHUNK + SUB), :]
            acc = win[SUB:SUB + CHUNK, :] * cw[CONV_W - 1:CONV_W, i * GW:(i + 1) * GW]
            for j in range(1, CONV_W):
                acc = acc + win[SUB - j:SUB - j + CHUNK, :] * cw[CONV_W - 1 - j:CONV_W - j, i * GW:(i + 1) * GW]
            conv.append(jax.nn.silu(acc))
        pst = _mm_nt(wsmallt, hbf_ref[pl.ds(r0, CHUNK), :])[0:2 * HEADS, :]
        o_g = _gdn_chunk(conv[0], conv[1], conv[2], ps_ref[pl.ds(r0, CHUNK), :], pst,
                         arow, dtrow, acol, dtcol, sg_ref, m)
        obuf_ref[pl.ds(r0, CHUNK), GW:2 * GW] = _gate_norm(o_g, p_ref[7, rows, :], gdnn).astype(bf16)
        return carry

    lax.fori_loop(0, tb // CHUNK, chunk, 0)

    p_ref[4:7, 0:SUB, :] = p_ref[4:7, tb:tb + SUB, :]

    out = jnp.dot(obuf_ref[...], wout_ref[...], preferred_element_type=f32)
    y_ref[...] = _rms(x + out, fnorm_ref[...])

    @pl.when(ti == pl.num_programs(1) - 1)
    def _():
        for h in range(HEADS):
            shg_ref[h] = st_ref[h].T
            sgdn_ref[h] = sg_ref[h]
        for i in range(3):
            tail = p_ref[4 + i, tb:tb + SUB, :]
            sconv_ref[:, i * GW:(i + 1) * GW] = tail[SUB - (CONV_W - 1):SUB, :]


def _const_spec(shape):
    nd = len(shape)
    return pl.BlockSpec(shape, lambda *_: (0,) * nd)


def _prompt_call(x, params):
    bsz, seq, d = x.shape
    tb = min(TOK_BLOCK, seq)
    assert seq % tb == 0 and tb % CHUNK == 0
    names = ("normw", "wmain", "wsmall", "wsmallt", "lbl", "convw", "arow", "dtrow", "acol", "dtcol",
             "hgn", "gdnn", "fnorm", "wout")
    consts = [params[k] for k in names]
    out_shape = (
        jax.ShapeDtypeStruct((bsz, seq, d), f32),
        jax.ShapeDtypeStruct((bsz, HEADS, HD, HD), f32),
        jax.ShapeDtypeStruct((bsz, HEADS, HD, HD), f32),
        jax.ShapeDtypeStruct((bsz, CONV_W - 1, 3 * GW), f32),
    )
    state_spec = pl.BlockSpec((None, HEADS, HD, HD), lambda b, t: (b, 0, 0, 0))
    return pl.pallas_call(
        _prompt_body,
        grid=(bsz, seq // tb),
        in_specs=[pl.BlockSpec((None, tb, d), lambda b, t: (b, t, 0))] + [_const_spec(c.shape) for c in consts],
        out_specs=(
            pl.BlockSpec((None, tb, d), lambda b, t: (b, t, 0)),
            state_spec, state_spec,
            pl.BlockSpec((None, CONV_W - 1, 3 * GW), lambda b, t: (b, 0, 0)),
        ),
        out_shape=out_shape,
        scratch_shapes=[
            pltpu.VMEM((N_GROUPS, tb + SUB, GW), f32),
            pltpu.VMEM((tb, 128), f32),
            pltpu.VMEM((tb, d), bf16),
            pltpu.VMEM((tb, 2 * GW), bf16),
            pltpu.VMEM((HEADS, HD, HD), f32),
            pltpu.VMEM((HEADS, HD, HD), f32),
        ],
        compiler_params=pltpu.CompilerParams(
            dimension_semantics=("arbitrary", "arbitrary"), vmem_limit_bytes=VMEM_LIMIT),
        name="prompt_mixer",
    )(x, *consts)


def _decode_body(x_ref, normw_ref, wmain_ref, wsmall_ref, lbl_ref, convw_ref, arow_ref, dtrow_ref,
                 hgn_ref, gdnn_ref, fnorm_ref, wout_ref, shg_in, sgdn_in, sconv_in,
                 y_ref, shg_out, sgdn_out, sconv_out,
                 p_ref, ps_ref, o_ref):
    i = pl.program_id(0)
    nb = shg_in.shape[0]

    @pl.when(i == 0)
    def _():
        hb = _rms(x_ref[...], normw_ref[...]).astype(bf16)
        for g in range(N_GROUPS):
            p_ref[g] = jnp.dot(hb, wmain_ref[g], preferred_element_type=f32)
        ps_ref[...] = jnp.dot(hb, wsmall_ref[...], preferred_element_type=f32)

    r0 = pl.multiple_of(i * nb, nb)
    rows = pl.ds(r0, nb)
    lbv = _lower_bound(lbl_ref[...])
    q_hg = p_ref[0, rows, :]
    fg = lbv + (1.0 - lbv) * jax.nn.sigmoid(p_ref[1, rows, :])
    v_hg = p_ref[2, rows, :]

    cw = convw_ref[...]
    cc = 3 * GW
    new = jnp.concatenate([p_ref[4, rows, :], p_ref[5, rows, :], p_ref[6, rows, :]], axis=1)
    acc = new * cw[CONV_W - 1:CONV_W, :]
    for j in range(CONV_W - 1):
        acc = acc + sconv_in[:, j * cc:(j + 1) * cc] * cw[j:j + 1, :]
    for j in range(CONV_W - 2):
        sconv_out[:, j * cc:(j + 1) * cc] = sconv_in[:, (j + 1) * cc:(j + 2) * cc]
    sconv_out[:, (CONV_W - 2) * cc:(CONV_W - 1) * cc] = new
    qkv = jax.nn.silu(acc)
    ps = ps_ref[rows, :]
    beta = jax.nn.sigmoid(ps)
    eg = jnp.exp(-jnp.exp(arow_ref[...]) * jax.nn.softplus(ps + dtrow_ref[...]))

    for h in range(HEADS):
        hs = slice(h * HD, (h + 1) * HD)
        qt = q_hg[:, hs].T
        ft = fg[:, hs].T
        qn = _l2n(qkv[:, hs]) * (HD ** -0.5)
        kn = _l2n(qkv[:, GW + h * HD:GW + (h + 1) * HD])
        qnt = qn.T
        knt = kn.T
        v_g = qkv[:, 2 * GW + h * HD:2 * GW + (h + 1) * HD]
        for j in range(nb):
            col = slice(j, j + 1)
            fcol = ft[:, col]
            s = fcol * shg_in[j, h] + (1.0 - fcol) * v_hg[j:j + 1, hs]
            shg_out[j, h] = s
            o_ref[i, j:j + 1, hs] = jnp.sum(qt[:, col] * s, axis=0, keepdims=True)

            kcol = knt[:, col]
            s = eg[j:j + 1, HEADS + h:HEADS + h + 1] * sgdn_in[j, h]
            delta = (v_g[j:j + 1, :] - jnp.sum(kcol * s, axis=0, keepdims=True)) * beta[j:j + 1, h:h + 1]
            s = s + kcol * delta
            sgdn_out[j, h] = s
            o_ref[i, j:j + 1, GW + h * HD:GW + (h + 1) * HD] = jnp.sum(qnt[:, col] * s, axis=0, keepdims=True)

    @pl.when(i == pl.num_programs(0) - 1)
    def _():
        o = o_ref[...].reshape(x_ref.shape[0], 2 * GW)
        og =jnp.concatenate([_gate_norm(o[:, :GW], p_ref[3], hgn_ref[...]),
                              _gate_norm(o[:, GW:], p_ref[7], gdnn_ref[...])], axis=1)
        out = jnp.dot(og.astype(bf16), wout_ref[...], preferred_element_type=f32)
        y_ref[...] = _rms(x_ref[...] + out, fnorm_ref[...])


def _decode_call(x, shg, sgdn, sconv, params):
    n, d = x.shape
    nb = min(DEC_BLOCK, n)
    assert n % nb == 0
    names = ("normw", "wmain", "wsmall", "lbl", "convw", "arow", "dtrow", "hgn", "gdnn", "fnorm", "wout")
    consts = [params[k] for k in names]
    state_spec = pl.BlockSpec((nb, HEADS, HD, HD), lambda i: (i, 0, 0, 0))
    conv_spec = pl.BlockSpec((nb, sconv.shape[1]), lambda i: (i, 0))
    return pl.pallas_call(
        _decode_body,
        grid=(n // nb,),
        in_specs=[_const_spec(x.shape)] + [_const_spec(c.shape) for c in consts]
        + [state_spec, state_spec, conv_spec],
        out_specs=(_const_spec(x.shape), state_spec, state_spec, conv_spec),
        out_shape=(
            jax.ShapeDtypeStruct((n, d), f32),
            jax.ShapeDtypeStruct(shg.shape, f32),
            jax.ShapeDtypeStruct(sgdn.shape, f32),
            jax.ShapeDtypeStruct(sconv.shape, f32),
        ),
        scratch_shapes=[
            pltpu.VMEM((N_GROUPS, n, GW), f32),
            pltpu.VMEM((n, 128), f32),
            pltpu.VMEM((n // nb, nb, 2 * GW), f32),
        ],
        compiler_params=pltpu.CompilerParams(
            dimension_semantics=("arbitrary",), vmem_limit_bytes=VMEM_LIMIT),
        name="decode_step",
    )(x, *consts, shg, sgdn, sconv)


def _prepare(norm_w, w_in, hg_lb_logits, conv_w, gdn_a_log, gdn_dt_bias, hg_out_norm, gdn_out_norm,
             w_out, final_norm):
    d = w_in.shape[1]
    w = w_in[0]
    nmain = N_GROUPS * GW
    small = w[:, nmain:]
    zrow = jnp.zeros((1, HEADS), f32)
    a = gdn_a_log[0].reshape(1, HEADS).astype(f32)
    dt = gdn_dt_bias[0].reshape(1, HEADS).astype(f32)
    return dict(
        normw=norm_w[0].reshape(1, d),
        wmain=w[:, :nmain].reshape(d, N_GROUPS, GW).transpose(1, 0, 2).astype(bf16),
        wsmall=jnp.pad(small, ((0, 0), (0, 128 - 2 * HEADS))).astype(bf16),
        wsmallt=jnp.pad(small.T, ((0, 16 - 2 * HEADS), (0, 0))).astype(bf16),
        lbl=hg_lb_logits.astype(f32),
        convw=conv_w[0].astype(f32),
        arow=jnp.pad(jnp.concatenate([zrow, a], axis=1), ((0, 0), (0, 128 - 2 * HEADS))),
        dtrow=jnp.pad(jnp.concatenate([zrow, dt], axis=1), ((0, 0), (0, 128 - 2 * HEADS))),
        acol=jnp.concatenate([zrow, a], axis=1).reshape(2 * HEADS, 1),
        dtcol=jnp.concatenate([zrow, dt], axis=1).reshape(2 * HEADS, 1),
        hgn=hg_out_norm[0].reshape(1, HD),
        gdnn=gdn_out_norm[0].reshape(1, HD),
        fnorm=final_norm.reshape(1, d),
        wout=w_out[0].astype(bf16),
    )


def kernel(x_prompt, x_sample, state_hgrn, state_gdn, state_gdn_conv, norm_w, w_in, hg_lb_logits, conv_w, gdn_a_log, gdn_dt_bias, hg_out_norm, gdn_out_norm, w_out, final_norm):
    assert w_in.shape[0] == 1, "single-layer kernel"
    params = _prepare(norm_w, w_in, hg_lb_logits, conv_w, gdn_a_log, gdn_dt_bias, hg_out_norm,
                      gdn_out_norm, w_out, final_norm)
    y_p, hg_p, gdn_p, conv_p = _prompt_call(x_prompt, params)
    n, dec_seq, d = x_sample.shape
    assert dec_seq == 1, "decode path is a single-token step"
    conv_shape = state_gdn_conv.shape[1:]
    y_s, hg_s, gdn_s, conv_s = _decode_call(x_sample.reshape(n, d), state_hgrn[0], state_gdn[0],
                                            state_gdn_conv[0].reshape(n, -1), params)
    return (y_p, y_s.reshape(n, 1, d), hg_p[None], gdn_p[None], conv_p[None],
            hg_s[None], gdn_s[None], conv_s.reshape(conv_shape)[None])
```

```python
import functools

import jax
import jax.numpy as jnp
from jax import lax
from jax.experimental import pallas as pl
from jax.experimental.pallas import tpu as pltpu

f32 = jnp.float32
bf16 = jnp.bfloat16

HEADS = 4
HD = 128
GW = HEADS * HD
N_GROUPS = 8
CONV_W = 4
CHUNK = 128
SUB = 8
EPS = 1e-6
TOK_BLOCK = 512
DEC_BLOCK = 8
VMEM_LIMIT = 56 * 1024 * 1024


def _mm(a, b):
    return jnp.dot(a.astype(bf16), b.astype(bf16), preferred_element_type=f32)


def _mm_nt(a, b):
    return lax.dot_general(a.astype(bf16), b.astype(bf16), (((1,), (1,)), ((), ())),
                           preferred_element_type=f32)


def _mm_tn(a, b):
    return lax.dot_general(a.astype(bf16), b.astype(bf16), (((0,), (0,)), ((), ())),
                           preferred_element_type=f32)


def _split3(x):
    a1 = x.astype(bf16)
    r1 = x - a1.astype(f32)
    a2 = r1.astype(bf16)
    r2 = r1 - a2.astype(f32)
    return a1, a2, r2.astype(bf16)


def _cumsum_rows(tril, x):
    a1, a2, a3 = _split3(x)
    d = lambda a: jnp.dot(tril, a, preferred_element_type=f32)
    return d(a1) + d(a2) + d(a3)


def _cumsum_lanes(x, triu):
    a1, a2, a3 = _split3(x)
    d = lambda a: jnp.dot(a, triu, preferred_element_type=f32)
    return d(a1) + d(a2) + d(a3)


def _rms(x, w):
    return x * lax.rsqrt(jnp.mean(x * x, axis=-1, keepdims=True) + EPS) * w


def _l2n(x):
    return x * lax.rsqrt(jnp.sum(x * x, axis=-1, keepdims=True) + EPS)


def _lower_bound(lbl):
    m = jnp.max(lbl, axis=0, keepdims=True)
    e = jnp.exp(lbl - m)
    return e[0:1, :] / jnp.sum(e, axis=0, keepdims=True)


def _chunk_masks():
    t = lax.broadcasted_iota(jnp.int32, (CHUNK, CHUNK), 0)
    s = lax.broadcasted_iota(jnp.int32, (CHUNK, CHUNK), 1)
    m = {}
    m["causal"] = s <= t
    m["strict"] = s < t
    m["eye"] = jnp.where(s == t, 1.0, 0.0).astype(f32)
    m["tril"] = jnp.where(s <= t, 1.0, 0.0).astype(bf16)
    m["triu"] = jnp.where(t <= s, 1.0, 0.0).astype(bf16)
    m["blk"] = (t // SUB) == (s // SUB)
    m["band"] = [(s == t - d) & ((t % SUB) >= d) for d in range(SUB)]
    lv = {}
    w = SUB
    while w < CHUNK:
        lv[w] = ((t // (2 * w)) == (s // (2 * w))) & (((t // w) % 2) == 1) & (((s // w) % 2) == 0)
        w *= 2
    m["lvl"] = lv
    return m


def _tri_inv(ns, m):
    eye = m["eye"]
    ps = [jnp.where(m["blk"], -n, 0.0) for n in ns]
    ts = [eye + p for p in ps]
    w = 2
    while w < SUB:
        ps = [_mm(p, p) for p in ps]
        ts = [_mm(t, eye + p) for t, p in zip(ts, ps)]
        w *= 2
    w = SUB
    while w < CHUNK:
        fts = [_mm(jnp.where(m["lvl"][w], n, 0.0), t) for n, t in zip(ns, ts)]
        ts = [t - _mm(t, ft) for t, ft in zip(ts, fts)]
        w *= 2
    return ts


def _hgrn_chunk(q, hf, v, lbv, st_ref, m):
    fg = lbv + (1.0 - lbv) * jax.nn.sigmoid(hf)
    kk = 1.0 - fg
    b = _cumsum_rows(m["tril"], jnp.log(fg))
    nsub = CHUNK // SUB
    b3 = b.reshape(nsub, SUB, GW)
    q3 = q.reshape(nsub, SUB, GW)
    k3 = kk.reshape(nsub, SUB, GW)
    last = [b3[j, SUB - 1:SUB, :] for j in range(nsub)]
    blast = last[nsub - 1]
    zero = jnp.zeros((SUB, GW), f32)

    lvl_q, lvl_k = {}, {}
    w = SUB
    while w < CHUNK:
        per = w // SUB
        qs, ks = [], []
        for j in range(nsub):
            blk = j // per
            if blk % 2 == 1:
                qs.append(q3[j] * jnp.exp(b3[j] - last[blk * per - 1]))
                ks.append(zero)
            else:
                qs.append(zero)
                ks.append(k3[j] * jnp.exp(last[blk * per + per - 1] - b3[j]))
        lvl_q[w] = jnp.concatenate(qs, axis=0)
        lvl_k[w] = jnp.concatenate(ks, axis=0)
        w *= 2

    fg3 = fg.reshape(nsub, SUB, GW)
    z = k3
    band = [q * kk]
    for d in range(1, SUB):
        z = fg3 * pltpu.roll(z, 1, 1)
        band.append((q3 * z).reshape(CHUNK, GW))

    qb = q * jnp.exp(b)
    kb = kk * jnp.exp(blast - b)
    eb = jnp.exp(blast)
    outs = []
    for h in range(HEADS):
        hs = slice(h * HD, (h + 1) * HD)
        sc = 0.0
        for d in range(SUB):
            sc = jnp.where(m["band"][d], jnp.sum(band[d][:, hs], axis=1, keepdims=True), sc)
        for w in lvl_q:
            sc = sc + jnp.where(m["lvl"][w], _mm_nt(lvl_q[w][:, hs], lvl_k[w][:, hs]), 0.0)
        st = st_ref[h]
        outs.append(_mm(sc, v[:, hs]) + _mm_nt(qb[:, hs], st))
        st_ref[h] = st * eb[:, hs] + _mm_tn(v[:, hs], kb[:, hs])
    return jnp.concatenate(outs, axis=1)


def _gdn_chunk(cq, ck, cv, ps, pst, arow, dtrow, acol, dtcol, s_ref, m):
    beta = jax.nn.sigmoid(ps)
    gcol = _cumsum_rows(m["tril"], -jnp.exp(arow) * jax.nn.softplus(ps + dtrow))
    grow = _cumsum_lanes(-jnp.exp(acol) * jax.nn.softplus(pst + dtcol), m["triu"])
    hr = range(HEADS)
    hsl = [slice(h * HD, (h + 1) * HD) for h in hr]
    qn = [_l2n(cq[:, hs]) * (HD ** -0.5) for hs in hsl]
    kn = [_l2n(ck[:, hs]) for hs in hsl]
    bc = [beta[:, h:h + 1] for h in hr]
    gc = [gcol[:, HEADS + h:HEADS + h + 1] for h in hr]
    lmat = [jnp.where(m["causal"], jnp.exp(gc[h] - grow[HEADS + h:HEADS + h + 1, :]), 0.0) for h in hr]
    kbeta = [kn[h] * bc[h] for h in hr]
    ns = [jnp.where(m["strict"], _mm_nt(kbeta[h], kn[h]) * lmat[h], 0.0) for h in hr]
    attn = [jnp.where(m["causal"], _mm_nt(qn[h], kn[h]) * lmat[h], 0.0) for h in hr]
    tinv = _tri_inv(ns, m)
    eg = [jnp.exp(g) for g in gc]
    sol = [_mm(tinv[h], jnp.concatenate([cv[:, hsl[h]] * bc[h], kbeta[h] * eg[h]], axis=1)) for h in hr]
    s = [s_ref[h] for h in hr]
    u = [sol[h][:, :HD] - _mm(sol[h][:, HD:], s[h]) for h in hr]
    outs = [_mm(qn[h] * eg[h], s[h]) + _mm(attn[h], u[h]) for h in hr]
    for h in hr:
        glast = gc[h][CHUNK - 1:CHUNK, :]
        s_ref[h] = jnp.exp(glast) * s[h] + _mm_tn(kn[h] * jnp.exp(glast - gc[h]), u[h])
    return jnp.concatenate(outs, axis=1)


def _gate_norm(o, gate, w):
    parts = []
    for h in range(HEADS):
        hs = slice(h * HD, (h + 1) * HD)
        parts.append(_rms(o[:, hs], w) * jax.nn.silu(gate[:, hs]))
    return jnp.concatenate(parts, axis=1)


def _prompt_body(x_ref, normw_ref, wmain_ref, wsmall_ref, wsmallt_ref, lbl_ref, convw_ref,
                 arow_ref, dtrow_ref, acol_ref, dtcol_ref, hgn_ref, gdnn_ref, fnorm_ref, wout_ref,
                 y_ref, shg_ref, sgdn_ref, sconv_ref,
                 p_ref, ps_ref, hbf_ref, obuf_ref, st_ref, sg_ref):
    ti = pl.program_id(1)
    tb = x_ref.shape[0]

    @pl.when(ti == 0)
    def _():
        st_ref[...] = jnp.zeros_like(st_ref)
        sg_ref[...] = jnp.zeros_like(sg_ref)
        p_ref[4:7, 0:SUB, :] = jnp.zeros((3, SUB, GW), f32)

    x = x_ref[...]
    hb = _rms(x, normw_ref[...]).astype(bf16)
    hbf_ref[...] = hb
    for g in range(N_GROUPS):
        p_ref[g, SUB:SUB + tb, :] = jnp.dot(hb, wmain_ref[g], preferred_element_type=f32)
    ps_ref[...] = jnp.dot(hb, wsmall_ref[...], preferred_element_type=f32)

    m = _chunk_masks()
    lbv = _lower_bound(lbl_ref[...])
    cw = convw_ref[...]
    arow, dtrow, acol, dtcol = arow_ref[...], dtrow_ref[...], acol_ref[...], dtcol_ref[...]
    hgn, gdnn = hgn_ref[...], gdnn_ref[...]
    wsmallt = wsmallt_ref[...]

    def chunk(c, carry):
        r0 = pl.multiple_of(c * CHUNK, CHUNK)
        rows = pl.ds(r0 + SUB, CHUNK)
        o_hg = _hgrn_chunk(p_ref[0, rows, :], p_ref[1, rows, :], p_ref[2, rows, :], lbv, st_ref, m)
        obuf_ref[pl.ds(r0, CHUNK), 0:GW] = _gate_norm(o_hg, p_ref[3, rows, :], hgn).astype(bf16)

        conv = []
        for i in range(3):
            win = p_ref[4 + i, pl.ds(r0, CHUNK + SUB), :]
            acc = win[SUB:SUB + CHUNK, :] * cw[CONV_W - 1:CONV_W, i * GW:(i + 1) * GW]
            for j in range(1, CONV_W):
                acc = acc + win[SUB - j:SUB - j + CHUNK, :] * cw[CONV_W - 1 - j:CONV_W - j, i * GW:(i + 1) * GW]
            conv.append(jax.nn.silu(acc))
        pst = _mm_nt(wsmallt, hbf_ref[pl.ds(r0, CHUNK), :])[0:2 * HEADS, :]
        o_g = _gdn_chunk(conv[0], conv[1], conv[2], ps_ref[pl.ds(r0, CHUNK), :], pst,
                         arow, dtrow, acol, dtcol, sg_ref, m)
        obuf_ref[pl.ds(r0, CHUNK), GW:2 * GW] = _gate_norm(o_g, p_ref[7, rows, :], gdnn).astype(bf16)
        return carry

    lax.fori_loop(0, tb // CHUNK, chunk, 0)

    p_ref[4:7, 0:SUB, :] = p_ref[4:7, tb:tb + SUB, :]

    out = jnp.dot(obuf_ref[...], wout_ref[...], preferred_element_type=f32)
    y_ref[...] = _rms(x + out, fnorm_ref[...])

    @pl.when(ti == pl.num_programs(1) - 1)
    def _():
        for h in range(HEADS):
            shg_ref[h] = st_ref[h].T
            sgdn_ref[h] = sg_ref[h]
        for i in range(3):
            tail = p_ref[4 + i, tb:tb + SUB, :]
            sconv_ref[:, i * GW:(i + 1) * GW] = tail[SUB - (CONV_W - 1):SUB, :]


def _const_spec(shape):
    nd = len(shape)
    return pl.BlockSpec(shape, lambda *_: (0,) * nd)


def _prompt_call(x, params):
    bsz, seq, d = x.shape
    tb = min(TOK_BLOCK, seq)
    assert seq % tb == 0 and tb % CHUNK == 0
    names = ("normw", "wmain", "wsmall", "wsmallt", "lbl", "convw", "arow", "dtrow", "acol", "dtcol",
             "hgn", "gdnn", "fnorm", "wout")
    consts = [params[k] for k in names]
    out_shape = (
        jax.ShapeDtypeStruct((bsz, seq, d), f32),
        jax.ShapeDtypeStruct((bsz, HEADS, HD, HD), f32),
        jax.ShapeDtypeStruct((bsz, HEADS, HD, HD), f32),
        jax.ShapeDtypeStruct((bsz, CONV_W - 1, 3 * GW), f32),
    )
    state_spec = pl.BlockSpec((None, HEADS, HD, HD), lambda b, t: (b, 0, 0, 0))
    return pl.pallas_call(
        _prompt_body,
        grid=(bsz, seq // tb),
        in_specs=[pl.BlockSpec((None, tb, d), lambda b, t: (b, t, 0))] + [_const_spec(c.shape) for c in consts],
        out_specs=(
            pl.BlockSpec((None, tb, d), lambda b, t: (b, t, 0)),
            state_spec, state_spec,
            pl.BlockSpec((None, CONV_W - 1, 3 * GW), lambda b, t: (b, 0, 0)),
        ),
        out_shape=out_shape,
        scratch_shapes=[
            pltpu.VMEM((N_GROUPS, tb + SUB, GW), f32),
            pltpu.VMEM((tb, 128), f32),
            pltpu.VMEM((tb, d), bf16),
            pltpu.VMEM((tb, 2 * GW), bf16),
            pltpu.VMEM((HEADS, HD, HD), f32),
            pltpu.VMEM((HEADS, HD, HD), f32),
        ],
        compiler_params=pltpu.CompilerParams(
            dimension_semantics=("arbitrary", "arbitrary"), vmem_limit_bytes=VMEM_LIMIT),
        name="prompt_mixer",
    )(x, *consts)


def _decode_body(x_ref, normw_ref, wmain_ref, wsmall_ref, lbl_ref, convw_ref, arow_ref, dtrow_ref,
                 hgn_ref, gdnn_ref, fnorm_ref, wout_ref, shg_in, sgdn_in, sconv_in,
                 y_ref, shg_out, sgdn_out, sconv_out,
                 p_ref, ps_ref, o_ref):
    i = pl.program_id(0)
    nb = shg_in.shape[0]

    @pl.when(i == 0)
    def _():
        hb = _rms(x_ref[...], normw_ref[...]).astype(bf16)
        for g in range(N_GROUPS):
            p_ref[g] = jnp.dot(hb, wmain_ref[g], preferred_element_type=f32)
        ps_ref[...] = jnp.dot(hb, wsmall_ref[...], preferred_element_type=f32)

    r0 = pl.multiple_of(i * nb, nb)
    rows = pl.ds(r0, nb)
    lbv = _lower_bound(lbl_ref[...])
    q_hg = p_ref[0, rows, :]
    fg = lbv + (1.0 - lbv) * jax.nn.sigmoid(p_ref[1, rows, :])
    v_hg = p_ref[2, rows, :]

    cw = convw_ref[...]
    cc = 3 * GW
    new = jnp.concatenate([p_ref[4, rows, :], p_ref[5, rows, :], p_ref[6, rows, :]], axis=1)
    acc = new * cw[CONV_W - 1:CONV_W, :]
    for j in range(CONV_W - 1):
        acc = acc + sconv_in[:, j * cc:(j + 1) * cc] * cw[j:j + 1, :]
    for j in range(CONV_W - 2):
        sconv_out[:, j * cc:(j + 1) * cc] = sconv_in[:, (j + 1) * cc:(j + 2) * cc]
    sconv_out[:, (CONV_W - 2) * cc:(CONV_W - 1) * cc] = new
    qkv = jax.nn.silu(acc)
    ps = ps_ref[rows, :]
    beta = jax.nn.sigmoid(ps)
    eg = jnp.exp(-jnp.exp(arow_ref[...]) * jax.nn.softplus(ps + dtrow_ref[...]))

    for h in range(HEADS):
        hs = slice(h * HD, (h + 1) * HD)
        qt = q_hg[:, hs].T
        ft = fg[:, hs].T
        qn = _l2n(qkv[:, hs]) * (HD ** -0.5)
        kn = _l2n(qkv[:, GW + h * HD:GW + (h + 1) * HD])
        qnt = qn.T
        knt = kn.T
        v_g = qkv[:, 2 * GW + h * HD:2 * GW + (h + 1) * HD]
        for j in range(nb):
            col = slice(j, j + 1)
            fcol = ft[:, col]
            s = fcol * shg_in[j, h] + (1.0 - fcol) * v_hg[j:j + 1, hs]
            shg_out[j, h] = s
            o_ref[i, j:j + 1, hs] = jnp.sum(qt[:, col] * s, axis=0, keepdims=True)

            kcol = knt[:, col]
            s = eg[j:j + 1, HEADS + h:HEADS + h + 1] * sgdn_in[j, h]
            delta = (v_g[j:j + 1, :] - jnp.sum(kcol * s, axis=0, keepdims=True)) * beta[j:j + 1, h:h + 1]
            s = s + kcol * delta
            sgdn_out[j, h] = s
            o_ref[i, j:j + 1, GW + h * HD:GW + (h + 1) * HD] = jnp.sum(qnt[:, col] * s, axis=0, keepdims=True)

    @pl.when(i == pl.num_programs(0) - 1)
    def _():
        o = o_ref[...].reshape(x_ref.shape[0], 2 * GW)
        og =jnp.concatenate([_gate_norm(o[:, :GW], p_ref[3], hgn_ref[...]),
                              _gate_norm(o[:, GW:], p_ref[7], gdnn_ref[...])], axis=1)
        out = jnp.dot(og.astype(bf16), wout_ref[...], preferred_element_type=f32)
        y_ref[...] = _rms(x_ref[...] + out, fnorm_ref[...])


def _decode_call(x, shg, sgdn, sconv, params):
    n, d = x.shape
    nb = min(DEC_BLOCK, n)
    assert n % nb == 0
    names = ("normw", "wmain", "wsmall", "lbl", "convw", "arow", "dtrow", "hgn", "gdnn", "fnorm", "wout")
    consts = [params[k] for k in names]
    state_spec = pl.BlockSpec((nb, HEADS, HD, HD), lambda i: (i, 0, 0, 0))
    conv_spec = pl.BlockSpec((nb, sconv.shape[1]), lambda i: (i, 0))
    return pl.pallas_call(
        _decode_body,
        grid=(n // nb,),
        in_specs=[_const_spec(x.shape)] + [_const_spec(c.shape) for c in consts]
        + [state_spec, state_spec, conv_spec],
        out_specs=(_const_spec(x.shape), state_spec, state_spec, conv_spec),
        out_shape=(
            jax.ShapeDtypeStruct((n, d), f32),
            jax.ShapeDtypeStruct(shg.shape, f32),
            jax.ShapeDtypeStruct(sgdn.shape, f32),
            jax.ShapeDtypeStruct(sconv.shape, f32),
        ),
        scratch_shapes=[
            pltpu.VMEM((N_GROUPS, n, GW), f32),
            pltpu.VMEM((n, 128), f32),
            pltpu.VMEM((n // nb, nb, 2 * GW), f32),
        ],
        compiler_params=pltpu.CompilerParams(
            dimension_semantics=("arbitrary",), vmem_limit_bytes=VMEM_LIMIT),
        name="decode_step",
    )(x, *consts, shg, sgdn, sconv)


def _prepare(norm_w, w_in, hg_lb_logits, conv_w, gdn_a_log, gdn_dt_bias, hg_out_norm, gdn_out_norm,
             w_out, final_norm):
    d = w_in.shape[1]
    w = w_in[0]
    nmain = N_GROUPS * GW
    small = w[:, nmain:]
    zrow = jnp.zeros((1, HEADS), f32)
    a = gdn_a_log[0].reshape(1, HEADS).astype(f32)
    dt = gdn_dt_bias[0].reshape(1, HEADS).astype(f32)
    return dict(
        normw=norm_w[0].reshape(1, d),
        wmain=w[:, :nmain].reshape(d, N_GROUPS, GW).transpose(1, 0, 2).astype(bf16),
        wsmall=jnp.pad(small, ((0, 0), (0, 128 - 2 * HEADS))).astype(bf16),
        wsmallt=jnp.pad(small.T, ((0, 16 - 2 * HEADS), (0, 0))).astype(bf16),
        lbl=hg_lb_logits.astype(f32),
        convw=conv_w[0].astype(f32),
        arow=jnp.pad(jnp.concatenate([zrow, a], axis=1), ((0, 0), (0, 128 - 2 * HEADS))),
        dtrow=jnp.pad(jnp.concatenate([zrow, dt], axis=1), ((0, 0), (0, 128 - 2 * HEADS))),
        acol=jnp.concatenate([zrow, a], axis=1).reshape(2 * HEADS, 1),
        dtcol=jnp.concatenate([zrow, dt], axis=1).reshape(2 * HEADS, 1),
        hgn=hg_out_norm[0].reshape(1, HD),
        gdnn=gdn_out_norm[0].reshape(1, HD),
        fnorm=final_norm.reshape(1, d),
        wout=w_out[0].astype(bf16),
    )


def kernel(x_prompt, x_sample, state_hgrn, state_gdn, state_gdn_conv, norm_w, w_in, hg_lb_logits, conv_w, gdn_a_log, gdn_dt_bias, hg_out_norm, gdn_out_norm, w_out, final_norm):
    assert w_in.shape[0] == 1, "single-layer kernel"
    params = _prepare(norm_w, w_in, hg_lb_logits, conv_w, gdn_a_log, gdn_dt_bias, hg_out_norm,
                      gdn_out_norm, w_out, final_norm)
    y_p, hg_p, gdn_p, conv_p = _prompt_call(x_prompt, params)
    n, dec_seq, d = x_sample.shape
    assert dec_seq == 1, "decode path is a single-token step"
    conv_shape = state_gdn_conv.shape[1:]
    y_s, hg_s, gdn_s, conv_s = _decode_call(x_sample.reshape(n, d), state_hgrn[0], state_gdn[0],
                                            state_gdn_conv[0].reshape(n, -1), params)
    return (y_p, y_s.reshape(n, 1, d), hg_p[None], gdn_p[None], conv_p[None],
            hg_s[None], gdn_s[None], conv_s.reshape(conv_shape)[None])
```

```python
import functools

import jax
import jax.numpy as jnp
from jax import lax
from jax.experimental import pallas as pl
from jax.experimental.pallas import tpu as pltpu

f32 = jnp.float32
bf16 = jnp.bfloat16

HEADS = 4
HD = 128
GW = HEADS * HD
N_GROUPS = 8
CONV_W = 4
CHUNK = 128
SUB = 8
EPS = 1e-6
TOK_BLOCK = 512
DEC_BLOCK = 8
VMEM_LIMIT = 56 * 1024 * 1024


def _mm(a, b):
    return jnp.dot(a.astype(bf16), b.astype(bf16), preferred_element_type=f32)


def _mm_nt(a, b):
    return lax.dot_general(a.astype(bf16), b.astype(bf16), (((1,), (1,)), ((), ())),
                           preferred_element_type=f32)


def _mm_tn(a, b):
    return lax.dot_general(a.astype(bf16), b.astype(bf16), (((0,), (0,)), ((), ())),
                           preferred_element_type=f32)


def _split3(x):
    a1 = x.astype(bf16)
    r1 = x - a1.astype(f32)
    a2 = r1.astype(bf16)
    r2 = r1 - a2.astype(f32)
    return a1, a2, r2.astype(bf16)


def _cumsum_rows(tril, x):
    a1, a2, a3 = _split3(x)
    d = lambda a: jnp.dot(tril, a, preferred_element_type=f32)
    return d(a1) + d(a2) + d(a3)


def _cumsum_lanes(x, triu):
    a1, a2, a3 = _split3(x)
    d = lambda a: jnp.dot(a, triu, preferred_element_type=f32)
    return d(a1) + d(a2) + d(a3)


def _rms(x, w):
    return x * lax.rsqrt(jnp.mean(x * x, axis=-1, keepdims=True) + EPS) * w


def _l2n(x):
    return x * lax.rsqrt(jnp.sum(x * x, axis=-1, keepdims=True) + EPS)


def _lower_bound(lbl):
    m = jnp.max(lbl, axis=0, keepdims=True)
    e = jnp.exp(lbl - m)
    return e[0:1, :] / jnp.sum(e, axis=0, keepdims=True)


def _chunk_masks():
    t = lax.broadcasted_iota(jnp.int32, (CHUNK, CHUNK), 0)
    s = lax.broadcasted_iota(jnp.int32, (CHUNK, CHUNK), 1)
    m = {}
    m["causal"] = s <= t
    m["strict"] = s < t
    m["eye"] = jnp.where(s == t, 1.0, 0.0).astype(f32)
    m["tril"] = jnp.where(s <= t, 1.0, 0.0).astype(bf16)
    m["triu"] = jnp.where(t <= s, 1.0, 0.0).astype(bf16)
    m["blk"] = (t // SUB) == (s // SUB)
    m["band"] = [(s == t - d) & ((t % SUB) >= d) for d in range(SUB)]
    lv = {}
    w = SUB
    while w < CHUNK:
        lv[w] = ((t // (2 * w)) == (s // (2 * w))) & (((t // w) % 2) == 1) & (((s // w) % 2) == 0)
        w *= 2
    m["lvl"] = lv
    return m


def _tri_inv(ns, m):
    eye = m["eye"]
    ps = [jnp.where(m["blk"], -n, 0.0) for n in ns]
    ts = [eye + p for p in ps]
    w = 2
    while w < SUB:
        ps = [_mm(p, p) for p in ps]
        yield
        ts = [_mm(t, eye + p) for t, p in zip(ts, ps)]
        yield
        w *= 2
    w = SUB
    while w < CHUNK:
        fts = [_mm(jnp.where(m["lvl"][w], n, 0.0), t) for n, t in zip(ns, ts)]
        yield
        ts = [t - _mm(t, ft) for t, ft in zip(ts, fts)]
        yield
        w *= 2
    return ts


def _interleave(*stages):
    live = list(stages)
    while live:
        for g in list(live):
            try:
                next(g)
            except StopIteration:
                live.remove(g)


def _hgrn_chunk(p_ref, rows, lbv, states, c, norm_w, store, m):
    q, hf, v = p_ref[0, rows, :], p_ref[1, rows, :], p_ref[2, rows, :]
    fg = lbv + (1.0 - lbv) * jax.nn.sigmoid(hf)
    kk = 1.0 - fg
    b = _cumsum_rows(m["tril"], jnp.log(fg))
    yield
    nsub = CHUNK // SUB
    b3 = b.reshape(nsub, SUB, GW)
    q3 = q.reshape(nsub, SUB, GW)
    k3 = kk.reshape(nsub, SUB, GW)
    last = [b3[j, SUB - 1:SUB, :] for j in range(nsub)]
    blast = last[nsub - 1]
    zero = jnp.zeros((SUB, GW), f32)

    lvl_q, lvl_k = {}, {}
    w = SUB
    while w < CHUNK:
        per = w // SUB
        qs, ks = [], []
        for j in range(nsub):
            blk = j // per
            if blk % 2 == 1:
                qs.append(q3[j] * jnp.exp(b3[j] - last[blk * per - 1]))
                ks.append(zero)
            else:
                qs.append(zero)
                ks.append(k3[j] * jnp.exp(last[blk * per + per - 1] - b3[j]))
        lvl_q[w] = jnp.concatenate(qs, axis=0)
        lvl_k[w] = jnp.concatenate(ks, axis=0)
        yield
        w *= 2

    fg3 = fg.reshape(nsub, SUB, GW)
    z = k3
    band = [q * kk]
    for d in range(1, SUB):
        z = fg3 * pltpu.roll(z, 1, 1)
        band.append((q3 * z).reshape(CHUNK, GW))
        yield

    qb = q * jnp.exp(b)
    kb = kk * jnp.exp(blast - b)
    eb = jnp.exp(blast)
    yield
    scs = []
    for h in range(HEADS):
        hs = slice(h * HD, (h + 1) * HD)
        sc = 0.0
        for d in range(SUB):
            sc = jnp.where(m["band"][d], jnp.sum(band[d][:, hs], axis=1, keepdims=True), sc)
        for w in lvl_q:
            sc = sc + jnp.where(m["lvl"][w], _mm_nt(lvl_q[w][:, hs], lvl_k[w][:, hs]), 0.0)
        scs.append(_mm(sc, v[:, hs]))
        yield
    hsl = [slice(h * HD, (h + 1) * HD) for h in range(HEADS)]
    inc = [_mm_tn(v[:, hs], kb[:, hs]) for hs in hsl]
    while states[c] is None:
        yield
    st = states[c]
    states[c + 1] = [st[h] * eb[:, hsl[h]] + inc[h] for h in range(HEADS)]
    yield
    for h in range(HEADS):
        o = scs[h] + _mm_nt(qb[:, hsl[h]], st[h])
        yield
        store(h, _rms(o, norm_w) * jax.nn.silu(p_ref[3, rows, hsl[h]]))
        yield


def _gdn_chunk(p_ref, r0, rows, ps, pst, cw, arow, dtrow, acol, dtcol, states, c, norm_w, store, m):
    hr = range(HEADS)
    hsl = [slice(h * HD, (h + 1) * HD) for h in hr]
    conv = []
    for i in range(3):
        win = p_ref[4 + i, pl.ds(r0, CHUNK + SUB), :]
        acc = win[SUB:SUB + CHUNK, :] * cw[CONV_W - 1:CONV_W, i * GW:(i + 1) * GW]
        for j in range(1, CONV_W):
            acc = acc + win[SUB - j:SUB - j + CHUNK, :] * cw[CONV_W - 1 - j:CONV_W - j, i * GW:(i + 1) * GW]
        conv.append(jax.nn.silu(acc))
        if i == 1:
            qn = [_l2n(conv[0][:, hs]) * (HD ** -0.5) for hs in hsl]
            kn = [_l2n(conv[1][:, hs]) for hs in hsl]
            beta = jax.nn.sigmoid(ps)
            gcol = _cumsum_rows(m["tril"], -jnp.exp(arow) * jax.nn.softplus(ps + dtrow))
            grow = _cumsum_lanes(-jnp.exp(acol) * jax.nn.softplus(pst + dtcol), m["triu"])
            bc = [beta[:, h:h + 1] for h in hr]
            gc = [gcol[:, HEADS + h:HEADS + h + 1] for h in hr]
            lmat = [jnp.where(m["causal"], jnp.exp(gc[h] - grow[HEADS + h:HEADS + h + 1, :]), 0.0) for h in hr]
            kbeta = [kn[h] * bc[h] for h in hr]
            ns = [jnp.where(m["strict"], _mm_nt(kbeta[h], kn[h]) * lmat[h], 0.0) for h in hr]
            tinv_stages = _tri_inv(ns, m)
        yield
    cv = conv[2]
    attn = [jnp.where(m["causal"], _mm_nt(qn[h], kn[h]) * lmat[h], 0.0) for h in hr]
    tinv = yield from tinv_stages
    eg = [jnp.exp(g) for g in gc]
    sol = [_mm(tinv[h], jnp.concatenate([cv[:, hsl[h]] * bc[h], kbeta[h] * eg[h]], axis=1)) for h in hr]
    yield
    kdec = [kn[h] * jnp.exp(gc[h][CHUNK - 1:CHUNK, :] - gc[h]) for h in hr]
    qdec = [qn[h] * eg[h] for h in hr]
    while states[c] is None:
        yield
    s = states[c]
    u = [sol[h][:, :HD] - _mm(sol[h][:, HD:], s[h]) for h in hr]
    yield
    states[c + 1] = [jnp.exp(gc[h][CHUNK - 1:CHUNK, :]) * s[h] + _mm_tn(kdec[h], u[h]) for h in hr]
    yield
    outs = [_mm(qdec[h], s[h]) + _mm(attn[h], u[h]) for h in hr]
    yield
    for h in hr:
        store(h, _rms(outs[h], norm_w) * jax.nn.silu(p_ref[7, rows, hsl[h]]))
        yield


def _gate_norm(o, gate, w):
    parts = []
    for h in range(HEADS):
        hs = slice(h * HD, (h + 1) * HD)
        parts.append(_rms(o[:, hs], w) * jax.nn.silu(gate[:, hs]))
    return jnp.concatenate(parts, axis=1)


def _prompt_body(x_ref, normw_ref, wmain_ref, wsmall_ref, wsmallt_ref, lbl_ref, convw_ref,
                 arow_ref, dtrow_ref, acol_ref, dtcol_ref, hgn_ref, gdnn_ref, fnorm_ref, wout_ref,
                 y_ref, shg_ref, sgdn_ref, sconv_ref,
                 p_ref, ps_ref, hbf_ref, obuf_ref, st_ref, sg_ref):
    ti = pl.program_id(1)
    tb = x_ref.shape[0]

    @pl.when(ti == 0)
    def _():
        st_ref[...] = jnp.zeros_like(st_ref)
        sg_ref[...] = jnp.zeros_like(sg_ref)
        p_ref[4:7, 0:SUB, :] = jnp.zeros((3, SUB, GW), f32)

    x = x_ref[...]
    hb = _rms(x, normw_ref[...]).astype(bf16)
    hbf_ref[...] = hb
    for g in range(N_GROUPS):
        p_ref[g, SUB:SUB + tb, :] = jnp.dot(hb, wmain_ref[g], preferred_element_type=f32)
    ps_ref[...] = jnp.dot(hb, wsmall_ref[...], preferred_element_type=f32)

    m = _chunk_masks()
    lbv = _lower_bound(lbl_ref[...])
    cw = convw_ref[...]
    arow, dtrow, acol, dtcol = arow_ref[...], dtrow_ref[...], acol_ref[...], dtcol_ref[...]
    hgn, gdnn = hgn_ref[...], gdnn_ref[...]
    wsmallt = wsmallt_ref[...]

    nchunk = tb // CHUNK
    hg_states = [[st_ref[h] for h in range(HEADS)]] + [None] * nchunk
    gdn_states = [[sg_ref[h] for h in range(HEADS)]] + [None] * nchunk

    def store_at(orows, col0):
        def store(h, val):
            obuf_ref[orows, col0 + h * HD:col0 + (h + 1) * HD] = val.astype(bf16)
        return store

    stages = []
    for j in range(nchunk):
        r0 = j * CHUNK
        rows = pl.ds(r0 + SUB, CHUNK)
        orows = pl.ds(r0, CHUNK)
        pst = _mm_nt(wsmallt, hbf_ref[orows, :])[0:2 * HEADS, :]
        stages.append(_gdn_chunk(p_ref, r0, rows, ps_ref[orows, :], pst, cw, arow, dtrow, acol, dtcol,
                                 gdn_states, j, gdnn, store_at(orows, GW), m))
        stages.append(_hgrn_chunk(p_ref, rows, lbv, hg_states, j, hgn, store_at(orows, 0), m))
    _interleave(*stages)
    for h in range(HEADS):
        st_ref[h] = hg_states[nchunk][h]
        sg_ref[h] = gdn_states[nchunk][h]

    p_ref[4:7, 0:SUB, :] = p_ref[4:7, tb:tb + SUB, :]

    out = jnp.dot(obuf_ref[...], wout_ref[...], preferred_element_type=f32)
    y_ref[...] = _rms(x + out, fnorm_ref[...])

    @pl.when(ti == pl.num_programs(1) - 1)
    def _():
        for h in range(HEADS):
            shg_ref[h] = st_ref[h].T
            sgdn_ref[h] = sg_ref[h]
        for i in range(3):
            tail = p_ref[4 + i, tb:tb + SUB, :]
            sconv_ref[:, i * GW:(i + 1) * GW] = tail[SUB - (CONV_W - 1):SUB, :]


def _const_spec(shape):
    nd = len(shape)
    return pl.BlockSpec(shape, lambda *_: (0,) * nd)


def _prompt_call(x, params):
    bsz, seq, d = x.shape
    tb = min(TOK_BLOCK, seq)
    assert seq % tb == 0 and tb % CHUNK == 0
    names = ("normw", "wmain", "wsmall", "wsmallt", "lbl", "convw", "arow", "dtrow", "acol", "dtcol",
             "hgn", "gdnn", "fnorm", "wout")
    consts = [params[k] for k in names]
    out_shape = (
        jax.ShapeDtypeStruct((bsz, seq, d), f32),
        jax.ShapeDtypeStruct((bsz, HEADS, HD, HD), f32),
        jax.ShapeDtypeStruct((bsz, HEADS, HD, HD), f32),
        jax.ShapeDtypeStruct((bsz, CONV_W - 1, 3 * GW), f32),
    )
    state_spec = pl.BlockSpec((None, HEADS, HD, HD), lambda b, t: (b, 0, 0, 0))
    return pl.pallas_call(
        _prompt_body,
        grid=(bsz, seq // tb),
        in_specs=[pl.BlockSpec((None, tb, d), lambda b, t: (b, t, 0))] + [_const_spec(c.shape) for c in consts],
        out_specs=(
            pl.BlockSpec((None, tb, d), lambda b, t: (b, t, 0)),
            state_spec, state_spec,
            pl.BlockSpec((None, CONV_W - 1, 3 * GW), lambda b, t: (b, 0, 0)),
        ),
        out_shape=out_shape,
        scratch_shapes=[
            pltpu.VMEM((N_GROUPS, tb + SUB, GW), f32),
            pltpu.VMEM((tb, 128), f32),
            pltpu.VMEM((tb, d), bf16),
            pltpu.VMEM((tb, 2 * GW), bf16),
            pltpu.VMEM((HEADS, HD, HD), f32),
            pltpu.VMEM((HEADS, HD, HD), f32),
        ],
        compiler_params=pltpu.CompilerParams(
            dimension_semantics=("arbitrary", "arbitrary"), vmem_limit_bytes=VMEM_LIMIT),
        name="prompt_mixer",
    )(x, *consts)


def _decode_body(x_ref, normw_ref, wmain_ref, wsmall_ref, lbl_ref, convw_ref, arow_ref, dtrow_ref,
                 hgn_ref, gdnn_ref, fnorm_ref, wout_ref, shg_in, sgdn_in, sconv_in,
                 y_ref, shg_out, sgdn_out, sconv_out,
                 p_ref, ps_ref, o_ref):
    i = pl.program_id(0)
    nb = shg_in.shape[0]

    @pl.when(i == 0)
    def _():
        hb = _rms(x_ref[...], normw_ref[...]).astype(bf16)
        for g in range(N_GROUPS):
            p_ref[g] = jnp.dot(hb, wmain_ref[g], preferred_element_type=f32)
        ps_ref[...] = jnp.dot(hb, wsmall_ref[...], preferred_element_type=f32)

    r0 = pl.multiple_of(i * nb, nb)
    rows = pl.ds(r0, nb)
    lbv = _lower_bound(lbl_ref[...])
    q_hg = p_ref[0, rows, :]
    fg = lbv + (1.0 - lbv) * jax.nn.sigmoid(p_ref[1, rows, :])
    v_hg = p_ref[2, rows, :]

    cw = convw_ref[...]
    cc = 3 * GW
    new = jnp.concatenate([p_ref[4, rows, :], p_ref[5, rows, :], p_ref[6, rows, :]], axis=1)
    acc = new * cw[CONV_W - 1:CONV_W, :]
    for j in range(CONV_W - 1):
        acc = acc + sconv_in[:, j * cc:(j + 1) * cc] * cw[j:j + 1, :]
    for j in range(CONV_W - 2):
        sconv_out[:, j * cc:(j + 1) * cc] = sconv_in[:, (j + 1) * cc:(j + 2) * cc]
    sconv_out[:, (CONV_W - 2) * cc:(CONV_W - 1) * cc] = new
    qkv = jax.nn.silu(acc)
    ps = ps_ref[rows, :]
    beta = jax.nn.sigmoid(ps)
    eg = jnp.exp(-jnp.exp(arow_ref[...]) * jax.nn.softplus(ps + dtrow_ref[...]))

    for h in range(HEADS):
        hs = slice(h * HD, (h + 1) * HD)
        qt = q_hg[:, hs].T
        ft = fg[:, hs].T
        qn = _l2n(qkv[:, hs]) * (HD ** -0.5)
        kn = _l2n(qkv[:, GW + h * HD:GW + (h + 1) * HD])
        qnt = qn.T
        knt = kn.T
        v_g = qkv[:, 2 * GW + h * HD:2 * GW + (h + 1) * HD]
        for j in range(nb):
            col = slice(j, j + 1)
            fcol = ft[:, col]
            s = fcol * shg_in[j, h] + (1.0 - fcol) * v_hg[j:j + 1, hs]
            shg_out[j, h] = s
            o_ref[i, j:j + 1, hs] = jnp.sum(qt[:, col] * s, axis=0, keepdims=True)

            kcol = knt[:, col]
            s = eg[j:j + 1, HEADS + h:HEADS + h + 1] * sgdn_in[j, h]
            delta = (v_g[j:j + 1, :] - jnp.sum(kcol * s, axis=0, keepdims=True)) * beta[j:j + 1, h:h + 1]
            s = s + kcol * delta
            sgdn_out[j, h] = s
            o_ref[i, j:j + 1, GW + h * HD:GW + (h + 1) * HD] = jnp.sum(qnt[:, col] * s, axis=0, keepdims=True)

    @pl.when(i == pl.num_programs(0) - 1)
    def _():
        o = o_ref[...].reshape(x_ref.shape[0], 2 * GW)
        og =jnp.concatenate([_gate_norm(o[:, :GW], p_ref[3], hgn_ref[...]),
                              _gate_norm(o[:, GW:], p_ref[7], gdnn_ref[...])], axis=1)
        out = jnp.dot(og.astype(bf16), wout_ref[...], preferred_element_type=f32)
        y_ref[...] = _rms(x_ref[...] + out, fnorm_ref[...])


def _decode_call(x, shg, sgdn, sconv, params):
    n, d = x.shape
    nb = min(DEC_BLOCK, n)
    assert n % nb == 0
    names = ("normw", "wmain", "wsmall", "lbl", "convw", "arow", "dtrow", "hgn", "gdnn", "fnorm", "wout")
    consts = [params[k] for k in names]
    state_spec = pl.BlockSpec((nb, HEADS, HD, HD), lambda i: (i, 0, 0, 0))
    conv_spec = pl.BlockSpec((nb, sconv.shape[1]), lambda i: (i, 0))
    return pl.pallas_call(
        _decode_body,
        grid=(n // nb,),
        in_specs=[_const_spec(x.shape)] + [_const_spec(c.shape) for c in consts]
        + [state_spec, state_spec, conv_spec],
        out_specs=(_const_spec(x.shape), state_spec, state_spec, conv_spec),
        out_shape=(
            jax.ShapeDtypeStruct((n, d), f32),
            jax.ShapeDtypeStruct(shg.shape, f32),
            jax.ShapeDtypeStruct(sgdn.shape, f32),
            jax.ShapeDtypeStruct(sconv.shape, f32),
        ),
        scratch_shapes=[
            pltpu.VMEM((N_GROUPS, n, GW), f32),
            pltpu.VMEM((n, 128), f32),
            pltpu.VMEM((n // nb, nb, 2 * GW), f32),
        ],
        compiler_params=pltpu.CompilerParams(
            dimension_semantics=("arbitrary",), vmem_limit_bytes=VMEM_LIMIT),
        name="decode_step",
    )(x, *consts, shg, sgdn, sconv)


def _prepare(norm_w, w_in, hg_lb_logits, conv_w, gdn_a_log, gdn_dt_bias, hg_out_norm, gdn_out_norm,
             w_out, final_norm):
    d = w_in.shape[1]
    w = w_in[0]
    nmain = N_GROUPS * GW
    small = w[:, nmain:]
    zrow = jnp.zeros((1, HEADS), f32)
    a = gdn_a_log[0].reshape(1, HEADS).astype(f32)
    dt = gdn_dt_bias[0].reshape(1, HEADS).astype(f32)
    return dict(
        normw=norm_w[0].reshape(1, d),
        wmain=w[:, :nmain].reshape(d, N_GROUPS, GW).transpose(1, 0, 2).astype(bf16),
        wsmall=jnp.pad(small, ((0, 0), (0, 128 - 2 * HEADS))).astype(bf16),
        wsmallt=jnp.pad(small.T, ((0, 16 - 2 * HEADS), (0, 0))).astype(bf16),
        lbl=hg_lb_logits.astype(f32),
        convw=conv_w[0].astype(f32),
        arow=jnp.pad(jnp.concatenate([zrow, a], axis=1), ((0, 0), (0, 128 - 2 * HEADS))),
        dtrow=jnp.pad(jnp.concatenate([zrow, dt], axis=1), ((0, 0), (0, 128 - 2 * HEADS))),
        acol=jnp.concatenate([zrow, a], axis=1).reshape(2 * HEADS, 1),
        dtcol=jnp.concatenate([zrow, dt], axis=1).reshape(2 * HEADS, 1),
        hgn=hg_out_norm[0].reshape(1, HD),
        gdnn=gdn_out_norm[0].reshape(1, HD),
        fnorm=final_norm.reshape(1, d),
        wout=w_out[0].astype(bf16),
    )


def kernel(x_prompt, x_sample, state_hgrn, state_gdn, state_gdn_conv, norm_w, w_in, hg_lb_logits, conv_w, gdn_a_log, gdn_dt_bias, hg_out_norm, gdn_out_norm, w_out, final_norm):
    assert w_in.shape[0] == 1, "single-layer kernel"
    params = _prepare(norm_w, w_in, hg_lb_logits, conv_w, gdn_a_log, gdn_dt_bias, hg_out_norm,
                      gdn_out_norm, w_out, final_norm)
    y_p, hg_p, gdn_p, conv_p = _prompt_call(x_prompt, params)
    n, dec_seq, d = x_sample.shape
    assert dec_seq == 1, "decode path is a single-token step"
    conv_shape = state_gdn_conv.shape[1:]
    y_s, hg_s, gdn_s, conv_s = _decode_call(x_sample.reshape(n, d), state_hgrn[0], state_gdn[0],
                                            state_gdn_conv[0].reshape(n, -1), params)
    return (y_p, y_s.reshape(n, 1, d), hg_p[None], gdn_p[None], conv_p[None],
            hg_s[None], gdn_s[None], conv_s.reshape(conv_shape)[None])
```

```python
import functools

import jax
import jax.numpy as jnp
from jax import lax
from jax.experimental import pallas as pl
from jax.experimental.pallas import tpu as pltpu

f32 = jnp.float32
bf16 = jnp.bfloat16

HEADS = 4
HD = 128
GW = HEADS * HD
N_GROUPS = 8
CONV_W = 4
CHUNK = 128
SUB = 8
EPS = 1e-6
TOK_BLOCK = 512
DEC_BLOCK = 8
VMEM_LIMIT = 56 * 1024 * 1024


def _mm(a, b):
    return jnp.dot(a.astype(bf16), b.astype(bf16), preferred_element_type=f32)


def _mm_nt(a, b):
    return lax.dot_general(a.astype(bf16), b.astype(bf16), (((1,), (1,)), ((), ())),
                           preferred_element_type=f32)


def _mm_tn(a, b):
    return lax.dot_general(a.astype(bf16), b.astype(bf16), (((0,), (0,)), ((), ())),
                           preferred_element_type=f32)


def _split3(x):
    a1 = x.astype(bf16)
    r1 = x - a1.astype(f32)
    a2 = r1.astype(bf16)
    r2 = r1 - a2.astype(f32)
    return a1, a2, r2.astype(bf16)


def _cumsum_rows(tril, x):
    a1, a2, a3 = _split3(x)
    d = lambda a: jnp.dot(tril, a, preferred_element_type=f32)
    return d(a1) + d(a2) + d(a3)


def _cumsum_lanes(x, triu):
    a1, a2, a3 = _split3(x)
    d = lambda a: jnp.dot(a, triu, preferred_element_type=f32)
    return d(a1) + d(a2) + d(a3)


def _rms(x, w):
    return x * lax.rsqrt(jnp.mean(x * x, axis=-1, keepdims=True) + EPS) * w


def _l2n(x):
    return x * lax.rsqrt(jnp.sum(x * x, axis=-1, keepdims=True) + EPS)


def _lower_bound(lbl):
    m = jnp.max(lbl, axis=0, keepdims=True)
    e = jnp.exp(lbl - m)
    return e[0:1, :] / jnp.sum(e, axis=0, keepdims=True)


def _chunk_masks():
    t = lax.broadcasted_iota(jnp.int32, (CHUNK, CHUNK), 0)
    s = lax.broadcasted_iota(jnp.int32, (CHUNK, CHUNK), 1)
    m = {}
    m["causal"] = s <= t
    m["strict"] = s < t
    m["eye"] = jnp.where(s == t, 1.0, 0.0).astype(f32)
    m["tril"] = jnp.where(s <= t, 1.0, 0.0).astype(bf16)
    m["triu"] = jnp.where(t <= s, 1.0, 0.0).astype(bf16)
    m["blk"] = (t // SUB) == (s // SUB)
    m["band"] = [(s == t - d) & ((t % SUB) >= d) for d in range(SUB)]
    lv = {}
    w = SUB
    while w < CHUNK:
        lv[w] = ((t // (2 * w)) == (s // (2 * w))) & (((t // w) % 2) == 1) & (((s // w) % 2) == 0)
        w *= 2
    m["lvl"] = lv
    return m


def _tri_inv(ns, m):
    eye = m["eye"]
    ps = [jnp.where(m["blk"], -n, 0.0) for n in ns]
    ts = [eye + p for p in ps]
    w = 2
    while w < SUB:
        ps = [_mm(p, p) for p in ps]
        yield
        ts = [_mm(t, eye + p) for t, p in zip(ts, ps)]
        yield
        w *= 2
    w = SUB
    while w < CHUNK:
        tbs = [t.astype(bf16) for t in ts]
        fts = [_mm(jnp.where(m["lvl"][w], n, 0.0), tb) for n, tb in zip(ns, tbs)]
        yield
        ts = [t - _mm(tb, ft) for t, tb, ft in zip(ts, tbs, fts)]
        yield
        w *= 2
    return ts


def _interleave(*stages):
    live = list(stages)
    while live:
        for g in list(live):
            try:
                next(g)
            except StopIteration:
                live.remove(g)


def _hgrn_chunk(p_ref, rows, lbv, states, c, norm_w, store, m):
    q, hf, v = p_ref[0, rows, :], p_ref[1, rows, :], p_ref[2, rows, :]
    fg = lbv + (1.0 - lbv) * jax.nn.sigmoid(hf)
    kk = 1.0 - fg
    b = _cumsum_rows(m["tril"], jnp.log(fg))
    yield
    nsub = CHUNK // SUB
    b3 = b.reshape(nsub, SUB, GW)
    q3 = q.reshape(nsub, SUB, GW)
    k3 = kk.reshape(nsub, SUB, GW)
    last = [b3[j, SUB - 1:SUB, :] for j in range(nsub)]
    blast = last[nsub - 1]
    zero = jnp.zeros((SUB, GW), f32)

    lvl_q, lvl_k = {}, {}
    w = SUB
    while w < CHUNK:
        per = w // SUB
        qs, ks = [], []
        for j in range(nsub):
            blk = j // per
            if blk % 2 == 1:
                qs.append(q3[j] * jnp.exp(b3[j] - last[blk * per - 1]))
                ks.append(zero)
            else:
                qs.append(zero)
                ks.append(k3[j] * jnp.exp(last[blk * per + per - 1] - b3[j]))
        lvl_q[w] = jnp.concatenate(qs, axis=0)
        lvl_k[w] = jnp.concatenate(ks, axis=0)
        yield
        w *= 2

    fg3 = fg.reshape(nsub, SUB, GW)
    z = k3
    band = [q * kk]
    for d in range(1, SUB):
        z = fg3 * pltpu.roll(z, 1, 1)
        band.append((q3 * z).reshape(CHUNK, GW))
        yield

    qb = q * jnp.exp(b)
    kb = kk * jnp.exp(blast - b)
    eb = jnp.exp(blast)
    yield
    scs = []
    for h in range(HEADS):
        hs = slice(h * HD, (h + 1) * HD)
        sc = 0.0
        for d in range(SUB):
            sc = jnp.where(m["band"][d], jnp.sum(band[d][:, hs], axis=1, keepdims=True), sc)
        for w in lvl_q:
            sc = jnp.where(m["lvl"][w], _mm_nt(lvl_q[w][:, hs], lvl_k[w][:, hs]), sc)
        scs.append(_mm(sc, v[:, hs]))
        yield
    hsl = [slice(h * HD, (h + 1) * HD) for h in range(HEADS)]
    inc = [_mm_tn(v[:, hs], kb[:, hs]) for hs in hsl]
    while states[c] is None:
        yield
    st = states[c]
    states[c + 1] = [st[h] * eb[:, hsl[h]] + inc[h] for h in range(HEADS)]
    yield
    for h in range(HEADS):
        o = scs[h] + _mm_nt(qb[:, hsl[h]], st[h])
        yield
        store(h, _rms(o, norm_w) * jax.nn.silu(p_ref[3, rows, hsl[h]]))
        yield


def _gdn_chunk(p_ref, r0, rows, ps, pst, cw, arow, dtrow, acol, dtcol, states, c, norm_w, store, m):
    hr = range(HEADS)
    hsl = [slice(h * HD, (h + 1) * HD) for h in hr]
    conv = []
    for i in range(3):
        win = p_ref[4 + i, pl.ds(r0, CHUNK + SUB), :]
        acc = win[SUB:SUB + CHUNK, :] * cw[CONV_W - 1:CONV_W, i * GW:(i + 1) * GW]
        for j in range(1, CONV_W):
            acc = acc + win[SUB - j:SUB - j + CHUNK, :] * cw[CONV_W - 1 - j:CONV_W - j, i * GW:(i + 1) * GW]
        conv.append(jax.nn.silu(acc))
        if i == 1:
            qn = [_l2n(conv[0][:, hs]) * (HD ** -0.5) for hs in hsl]
            kn = [_l2n(conv[1][:, hs]) for hs in hsl]
            beta = jax.nn.sigmoid(ps)
            gcol = _cumsum_rows(m["tril"], -jnp.exp(arow) * jax.nn.softplus(ps + dtrow))
            grow = _cumsum_lanes(-jnp.exp(acol) * jax.nn.softplus(pst + dtcol), m["triu"])
            bc = [beta[:, h:h + 1] for h in hr]
            gc = [gcol[:, HEADS + h:HEADS + h + 1] for h in hr]
            lmat = [jnp.where(m["causal"], jnp.exp(gc[h] - grow[HEADS + h:HEADS + h + 1, :]), 0.0) for h in hr]
            kbeta = [kn[h] * bc[h] for h in hr]
            ns = [jnp.where(m["strict"], _mm_nt(kbeta[h], kn[h]) * lmat[h], 0.0) for h in hr]
            tinv_stages = _tri_inv(ns, m)
        yield
    cv = conv[2]
    attn = [jnp.where(m["causal"], _mm_nt(qn[h], kn[h]) * lmat[h], 0.0) for h in hr]
    tinv = yield from tinv_stages
    eg = [jnp.exp(g) for g in gc]
    sol = [_mm(tinv[h], jnp.concatenate([cv[:, hsl[h]] * bc[h], kbeta[h] * eg[h]], axis=1)) for h in hr]
    yield
    kdec = [kn[h] * jnp.exp(gc[h][CHUNK - 1:CHUNK, :] - gc[h]) for h in hr]
    qdec = [qn[h] * eg[h] for h in hr]
    while states[c] is None:
        yield
    s = states[c]
    u = [sol[h][:, :HD] - _mm(sol[h][:, HD:], s[h]) for h in hr]
    yield
    states[c + 1] = [jnp.exp(gc[h][CHUNK - 1:CHUNK, :]) * s[h] + _mm_tn(kdec[h], u[h]) for h in hr]
    yield
    outs = [_mm(qdec[h], s[h]) + _mm(attn[h], u[h]) for h in hr]
    yield
    for h in hr:
        store(h, _rms(outs[h], norm_w) * jax.nn.silu(p_ref[7, rows, hsl[h]]))
        yield


def _gate_norm(o, gate, w):
    parts = []
    for h in range(HEADS):
        hs = slice(h * HD, (h + 1) * HD)
        parts.append(_rms(o[:, hs], w) * jax.nn.silu(gate[:, hs]))
    return jnp.concatenate(parts, axis=1)


def _prompt_body(x_ref, normw_ref, wmain_ref, wsmall_ref, wsmallt_ref, lbl_ref, convw_ref,
                 arow_ref, dtrow_ref, acol_ref, dtcol_ref, hgn_ref, gdnn_ref, fnorm_ref, wout_ref,
                 y_ref, shg_ref, sgdn_ref, sconv_ref,
                 p_ref, ps_ref, hbf_ref, obuf_ref, st_ref, sg_ref):
    ti = pl.program_id(1)
    tb = x_ref.shape[0]

    @pl.when(ti == 0)
    def _():
        st_ref[...] = jnp.zeros_like(st_ref)
        sg_ref[...] = jnp.zeros_like(sg_ref)
        p_ref[4:7, 0:SUB, :] = jnp.zeros((3, SUB, GW), f32)

    x = x_ref[...]
    hb = _rms(x, normw_ref[...]).astype(bf16)
    hbf_ref[...] = hb
    for g in range(N_GROUPS):
        p_ref[g, SUB:SUB + tb, :] = jnp.dot(hb, wmain_ref[:, g * GW:(g + 1) * GW], preferred_element_type=f32)
    ps_ref[...] = jnp.dot(hb, wsmall_ref[...], preferred_element_type=f32)

    m = _chunk_masks()
    lbv = _lower_bound(lbl_ref[...])
    cw = convw_ref[...]
    arow, dtrow, acol, dtcol = arow_ref[...], dtrow_ref[...], acol_ref[...], dtcol_ref[...]
    hgn, gdnn = hgn_ref[...], gdnn_ref[...]
    wsmallt = wsmallt_ref[...]

    nchunk = tb // CHUNK
    hg_states = [[st_ref[h] for h in range(HEADS)]] + [None] * nchunk
    gdn_states = [[sg_ref[h] for h in range(HEADS)]] + [None] * nchunk

    def store_at(orows, col0):
        def store(h, val):
            obuf_ref[orows, col0 + h * HD:col0 + (h + 1) * HD] = val.astype(bf16)
        return store

    stages = []
    for j in range(nchunk):
        r0 = j * CHUNK
        rows = pl.ds(r0 + SUB, CHUNK)
        orows = pl.ds(r0, CHUNK)
        pst = _mm_nt(wsmallt, hbf_ref[orows, :])[0:2 * HEADS, :]
        stages.append(_gdn_chunk(p_ref, r0, rows, ps_ref[orows, :], pst, cw, arow, dtrow, acol, dtcol,
                                 gdn_states, j, gdnn, store_at(orows, GW), m))
        stages.append(_hgrn_chunk(p_ref, rows, lbv, hg_states, j, hgn, store_at(orows, 0), m))
    _interleave(*stages)
    for h in range(HEADS):
        st_ref[h] = hg_states[nchunk][h]
        sg_ref[h] = gdn_states[nchunk][h]

    p_ref[4:7, 0:SUB, :] = p_ref[4:7, tb:tb + SUB, :]

    out = jnp.dot(obuf_ref[...], wout_ref[...], preferred_element_type=f32)
    y_ref[...] = _rms(x + out, fnorm_ref[...])

    @pl.when(ti == pl.num_programs(1) - 1)
    def _():
        for h in range(HEADS):
            shg_ref[h] = st_ref[h].T
            sgdn_ref[h] = sg_ref[h]
        for i in range(3):
            tail = p_ref[4 + i, tb:tb + SUB, :]
            sconv_ref[:, i * GW:(i + 1) * GW] = tail[SUB - (CONV_W - 1):SUB, :]


def _const_spec(shape):
    nd = len(shape)
    return pl.BlockSpec(shape, lambda *_: (0,) * nd)


def _prompt_call(x, params):
    bsz, seq, d = x.shape
    tb = min(TOK_BLOCK, seq)
    assert seq % tb == 0 and tb % CHUNK == 0
    names = ("normw", "wmain", "wsmall", "wsmallt", "lbl", "convw", "arow", "dtrow", "acol", "dtcol",
             "hgn", "gdnn", "fnorm", "wout")
    consts = [params[k] for k in names]
    out_shape = (
        jax.ShapeDtypeStruct((bsz, seq, d), f32),
        jax.ShapeDtypeStruct((bsz, HEADS, HD, HD), f32),
        jax.ShapeDtypeStruct((bsz, HEADS, HD, HD), f32),
        jax.ShapeDtypeStruct((bsz, CONV_W - 1, 3 * GW), f32),
    )
    state_spec = pl.BlockSpec((None, HEADS, HD, HD), lambda b, t: (b, 0, 0, 0))
    return pl.pallas_call(
        _prompt_body,
        grid=(bsz, seq // tb),
        in_specs=[pl.BlockSpec((None, tb, d), lambda b, t: (b, t, 0))] + [_const_spec(c.shape) for c in consts],
        out_specs=(
            pl.BlockSpec((None, tb, d), lambda b, t: (b, t, 0)),
            state_spec, state_spec,
            pl.BlockSpec((None, CONV_W - 1, 3 * GW), lambda b, t: (b, 0, 0)),
        ),
        out_shape=out_shape,
        scratch_shapes=[
            pltpu.VMEM((N_GROUPS, tb + SUB, GW), f32),
            pltpu.VMEM((tb, 128), f32),
            pltpu.VMEM((tb, d), bf16),
            pltpu.VMEM((tb, 2 * GW), bf16),
            pltpu.VMEM((HEADS, HD, HD), f32),
            pltpu.VMEM((HEADS, HD, HD), f32),
        ],
        compiler_params=pltpu.CompilerParams(
            dimension_semantics=("arbitrary", "arbitrary"), vmem_limit_bytes=VMEM_LIMIT),
        name="prompt_mixer",
    )(x, *consts)


def _decode_body(x_ref, normw_ref, wmain_ref, wsmall_ref, lbl_ref, convw_ref, arow_ref, dtrow_ref,
                 hgn_ref, gdnn_ref, fnorm_ref, wout_ref, shg_in, sgdn_in, sconv_in,
                 y_ref, shg_out, sgdn_out, sconv_out,
                 p_ref, ps_ref, o_ref):
    i = pl.program_id(0)
    nb = shg_in.shape[0]

    @pl.when(i == 0)
    def _():
        hb = _rms(x_ref[...], normw_ref[...]).astype(bf16)
        for g in range(N_GROUPS):
            p_ref[g] = jnp.dot(hb, wmain_ref[:, g * GW:(g + 1) * GW], preferred_element_type=f32)
        ps_ref[...] = jnp.dot(hb, wsmall_ref[...], preferred_element_type=f32)

    r0 = pl.multiple_of(i * nb, nb)
    rows = pl.ds(r0, nb)
    lbv = _lower_bound(lbl_ref[...])
    q_hg = p_ref[0, rows, :]
    fg = lbv + (1.0 - lbv) * jax.nn.sigmoid(p_ref[1, rows, :])
    v_hg = p_ref[2, rows, :]

    cw = convw_ref[...]
    cc = 3 * GW
    new = jnp.concatenate([p_ref[4, rows, :], p_ref[5, rows, :], p_ref[6, rows, :]], axis=1)
    acc = new * cw[CONV_W - 1:CONV_W, :]
    for j in range(CONV_W - 1):
        acc = acc + sconv_in[:, j * cc:(j + 1) * cc] * cw[j:j + 1, :]
    for j in range(CONV_W - 2):
        sconv_out[:, j * cc:(j + 1) * cc] = sconv_in[:, (j + 1) * cc:(j + 2) * cc]
    sconv_out[:, (CONV_W - 2) * cc:(CONV_W - 1) * cc] = new
    qkv = jax.nn.silu(acc)
    ps = ps_ref[rows, :]
    beta = jax.nn.sigmoid(ps)
    eg = jnp.exp(-jnp.exp(arow_ref[...]) * jax.nn.softplus(ps + dtrow_ref[...]))

    pairs = [(h, j) for h in range(HEADS) for j in range(nb)]
    hsl = [slice(h * HD, (h + 1) * HD) for h in range(HEADS)]
    ft = [fg[:, hs].T for hs in hsl]
    q_h = [q_hg[:, hs].astype(bf16) for hs in hsl]
    kn = [_l2n(qkv[:, GW + h * HD:GW + (h + 1) * HD]) for h in range(HEADS)]
    knt = [k.T for k in kn]
    kn_b = [k.astype(bf16) for k in kn]
    qn = [(_l2n(qkv[:, hs]) * (HD ** -0.5)).astype(bf16) for hs in hsl]
    for h, j in pairs:
        fcol = ft[h][:, j:j + 1]
        s = fcol * shg_in[j, h] + (1.0 - fcol) * v_hg[j:j + 1, hsl[h]]
        shg_out[j, h] = s
        o_ref[i, j:j + 1, hsl[h]] = _mm(q_h[h], s)[j:j + 1, :]
    ks = {}
    for h, j in pairs:
        s = eg[j:j + 1, HEADS + h:HEADS + h + 1] * sgdn_in[j, h]
        sgdn_out[j, h] = s
        ks[h, j] = _mm(kn_b[h], s)[j:j + 1, :]
    for h, j in pairs:
        v_row = qkv[j:j + 1, 2 * GW + h * HD:2 * GW + (h + 1) * HD]
        delta = (v_row - ks[h, j]) * beta[j:j + 1, h:h + 1]
        s = sgdn_out[j, h] + knt[h][:, j:j + 1] * delta
        sgdn_out[j, h] = s
        o_ref[i, j:j + 1, GW + h * HD:GW + (h + 1) * HD] = _mm(qn[h], s)[j:j + 1, :]

    @pl.when(i == pl.num_programs(0) - 1)
    def _():
        o = o_ref[...].reshape(x_ref.shape[0], 2 * GW)
        og = jnp.concatenate([_gate_norm(o[:, :GW], p_ref[3], hgn_ref[...]),
                              _gate_norm(o[:, GW:], p_ref[7], gdnn_ref[...])], axis=1)
        out = jnp.dot(og.astype(bf16), wout_ref[...], preferred_element_type=f32)
        y_ref[...] = _rms(x_ref[...] + out, fnorm_ref[...])


def _decode_call(x, shg, sgdn, sconv, params):
    n, d = x.shape
    nb = min(DEC_BLOCK, n)
    assert n % nb == 0
    names = ("normw", "wmain", "wsmall", "lbl", "convw", "arow", "dtrow", "hgn", "gdnn", "fnorm", "wout")
    consts = [params[k] for k in names]
    state_spec = pl.BlockSpec((nb, HEADS, HD, HD), lambda i: (i, 0, 0, 0))
    conv_spec = pl.BlockSpec((nb, sconv.shape[1]), lambda i: (i, 0))
    return pl.pallas_call(
        _decode_body,
        grid=(n // nb,),
        in_specs=[_const_spec(x.shape)] + [_const_spec(c.shape) for c in consts]
        + [state_spec, state_spec, conv_spec],
        out_specs=(_const_spec(x.shape), state_spec, state_spec, conv_spec),
        out_shape=(
            jax.ShapeDtypeStruct((n, d), f32),
            jax.ShapeDtypeStruct(shg.shape, f32),
            jax.ShapeDtypeStruct(sgdn.shape, f32),
            jax.ShapeDtypeStruct(sconv.shape, f32),
        ),
        scratch_shapes=[
            pltpu.VMEM((N_GROUPS, n, GW), f32),
            pltpu.VMEM((n, 128), f32),
            pltpu.VMEM((n // nb, nb, 2 * GW), f32),
        ],
        compiler_params=pltpu.CompilerParams(
            dimension_semantics=("arbitrary",), vmem_limit_bytes=VMEM_LIMIT),
        name="decode_step",
    )(x, *consts, shg, sgdn, sconv)


def _prepare(norm_w, w_in, hg_lb_logits, conv_w, gdn_a_log, gdn_dt_bias, hg_out_norm, gdn_out_norm,
             w_out, final_norm):
    d = w_in.shape[1]
    w = w_in[0]
    nmain = N_GROUPS * GW
    small = w[:, nmain:]
    zrow = jnp.zeros((1, HEADS), f32)
    a = gdn_a_log[0].reshape(1, HEADS).astype(f32)
    dt = gdn_dt_bias[0].reshape(1, HEADS).astype(f32)
    return dict(
        normw=norm_w[0].reshape(1, d),
        wmain=w.astype(bf16),
        wsmall=jnp.pad(small, ((0, 0), (0, 128 - 2 * HEADS))).astype(bf16),
        wsmallt=jnp.pad(small.T, ((0, 16 - 2 * HEADS), (0, 0))).astype(bf16),
        lbl=hg_lb_logits.astype(f32),
        convw=conv_w[0].astype(f32),
        arow=jnp.pad(jnp.concatenate([zrow, a], axis=1), ((0, 0), (0, 128 - 2 * HEADS))),
        dtrow=jnp.pad(jnp.concatenate([zrow, dt], axis=1), ((0, 0), (0, 128 - 2 * HEADS))),
        acol=jnp.concatenate([zrow, a], axis=1).reshape(2 * HEADS, 1),
        dtcol=jnp.concatenate([zrow, dt], axis=1).reshape(2 * HEADS, 1),
        hgn=hg_out_norm[0].reshape(1, HD),
        gdnn=gdn_out_norm[0].reshape(1, HD),
        fnorm=final_norm.reshape(1, d),
        wout=w_out[0].astype(bf16),
    )


def kernel(x_prompt, x_sample, state_hgrn, state_gdn, state_gdn_conv, norm_w, w_in, hg_lb_logits, conv_w, gdn_a_log, gdn_dt_bias, hg_out_norm, gdn_out_norm, w_out, final_norm):
    assert w_in.shape[0] == 1, "single-layer kernel"
    params = _prepare(norm_w, w_in, hg_lb_logits, conv_w, gdn_a_log, gdn_dt_bias, hg_out_norm,
                      gdn_out_norm, w_out, final_norm)
    y_p, hg_p, gdn_p, conv_p = _prompt_call(x_prompt, params)
    n, dec_seq, d = x_sample.shape
    assert dec_seq == 1, "decode path is a single-token step"
    conv_shape = state_gdn_conv.shape[1:]
    y_s, hg_s, gdn_s, conv_s = _decode_call(x_sample.reshape(n, d), state_hgrn[0], state_gdn[0],
                                            state_gdn_conv[0].reshape(n, -1), params)
    return (y_p, y_s.reshape(n, 1, d), hg_p[None], gdn_p[None], conv_p[None],
            hg_s[None], gdn_s[None], conv_s.reshape(conv_shape)[None])
```

```python
import functools

import jax
import jax.numpy as jnp
from jax import lax
from jax.experimental import pallas as pl
from jax.experimental.pallas import tpu as pltpu

f32 = jnp.float32
bf16 = jnp.bfloat16

HEADS = 4
HD = 128
GW = HEADS * HD
N_GROUPS = 8
CONV_W = 4
CHUNK = 128
SUB = 8
EPS = 1e-6
TOK_BLOCK = 512
DEC_BLOCK = 8
PROJ_RATE = 4
HGRN_LAG = 1
VMEM_LIMIT = 56 * 1024 * 1024


def _mm(a, b):
    return jnp.dot(a.astype(bf16), b.astype(bf16), preferred_element_type=f32)


def _mm_nt(a, b):
    return lax.dot_general(a.astype(bf16), b.astype(bf16), (((1,), (1,)), ((), ())),
                           preferred_element_type=f32)


def _mm_tn(a, b):
    return lax.dot_general(a.astype(bf16), b.astype(bf16), (((0,), (0,)), ((), ())),
                           preferred_element_type=f32)


def _split3(x):
    a1 = x.astype(bf16)
    r1 = x - a1.astype(f32)
    a2 = r1.astype(bf16)
    r2 = r1 - a2.astype(f32)
    return a1, a2, r2.astype(bf16)


def _cumsum_rows(tril, x):
    a1, a2, a3 = _split3(x)
    d = lambda a: jnp.dot(tril, a, preferred_element_type=f32)
    return d(a1) + d(a2) + d(a3)


def _cumsum_lanes(x, triu):
    a1, a2, a3 = _split3(x)
    d = lambda a: jnp.dot(a, triu, preferred_element_type=f32)
    return d(a1) + d(a2) + d(a3)


def _rms(x, w):
    return x * lax.rsqrt(jnp.mean(x * x, axis=-1, keepdims=True) + EPS) * w


def _l2n(x):
    return x * lax.rsqrt(jnp.sum(x * x, axis=-1, keepdims=True) + EPS)


def _lower_bound(lbl):
    m = jnp.max(lbl, axis=0, keepdims=True)
    e = jnp.exp(lbl - m)
    return e[0:1, :] / jnp.sum(e, axis=0, keepdims=True)


def _chunk_masks():
    t = lax.broadcasted_iota(jnp.int32, (CHUNK, CHUNK), 0)
    s = lax.broadcasted_iota(jnp.int32, (CHUNK, CHUNK), 1)
    m = {}
    m["causal"] = s <= t
    m["strict"] = s < t
    m["eye"] = jnp.where(s == t, 1.0, 0.0).astype(f32)
    m["tril"] = jnp.where(s <= t, 1.0, 0.0).astype(bf16)
    m["triu"] = jnp.where(t <= s, 1.0, 0.0).astype(bf16)
    m["blk"] = (t // SUB) == (s // SUB)
    m["band"] = [(s == t - d) & ((t % SUB) >= d) for d in range(SUB)]
    lv = {}
    w = SUB
    while w < CHUNK:
        lv[w] = ((t // (2 * w)) == (s // (2 * w))) & (((t // w) % 2) == 1) & (((s // w) % 2) == 0)
        w *= 2
    m["lvl"] = lv
    return m


def _tri_inv(ns, m):
    eye = m["eye"]
    ps = [jnp.where(m["blk"], -n, 0.0) for n in ns]
    ts = [eye + p for p in ps]
    w = 2
    while w < SUB:
        ps = [_mm(p, p) for p in ps]
        yield
        ts = [_mm(t, eye + p) for t, p in zip(ts, ps)]
        yield
        w *= 2
    w = SUB
    while w < CHUNK:
        tbs = [t.astype(bf16) for t in ts]
        fts = [_mm(jnp.where(m["lvl"][w], n, 0.0), tb) for n, tb in zip(ns, tbs)]
        yield
        ts = [t - _mm(tb, ft) for t, tb, ft in zip(ts, tbs, fts)]
        yield
        w *= 2
    return ts


def _interleave(stages, starts):
    pending = sorted(zip(starts, range(len(stages))))
    live, turn = [], 0
    while live or pending:
        while pending and pending[0][0] <= turn:
            live.append(stages[pending.pop(0)[1]])
        for g in list(live):
            try:
                next(g)
            except StopIteration:
                live.remove(g)
        turn += 1


def _hgrn_chunk(p_ref, rows, lbv, states, c, norm_w, store, m):
    q, hf, v = p_ref[0, rows, :], p_ref[1, rows, :], p_ref[2, rows, :]
    fg = lbv + (1.0 - lbv) * jax.nn.sigmoid(hf)
    kk = 1.0 - fg
    b = _cumsum_rows(m["tril"], jnp.log(fg))
    yield
    nsub = CHUNK // SUB
    b3 = b.reshape(nsub, SUB, GW)
    q3 = q.reshape(nsub, SUB, GW)
    k3 = kk.reshape(nsub, SUB, GW)
    last = [b3[j, SUB - 1:SUB, :] for j in range(nsub)]
    blast = last[nsub - 1]
    zero = jnp.zeros((SUB, GW), f32)

    lvl_q, lvl_k = {}, {}
    w = SUB
    while w < CHUNK:
        per = w // SUB
        qs, ks = [], []
        for j in range(nsub):
            blk = j // per
            if blk % 2 == 1:
                qs.append(q3[j] * jnp.exp(b3[j] - last[blk * per - 1]))
                ks.append(zero)
            else:
                qs.append(zero)
                ks.append(k3[j] * jnp.exp(last[blk * per + per - 1] - b3[j]))
        lvl_q[w] = jnp.concatenate(qs, axis=0)
        lvl_k[w] = jnp.concatenate(ks, axis=0)
        yield
        w *= 2

    fg3 = fg.reshape(nsub, SUB, GW)
    z = k3
    band = [q * kk]
    for d in range(1, SUB):
        z = fg3 * pltpu.roll(z, 1, 1)
        band.append((q3 * z).reshape(CHUNK, GW))
        yield

    qb = q * jnp.exp(b)
    kb = kk * jnp.exp(blast - b)
    eb = jnp.exp(blast)
    yield
    scs = []
    for h in range(HEADS):
        hs = slice(h * HD, (h + 1) * HD)
        sc = 0.0
        for d in range(SUB):
            sc = jnp.where(m["band"][d], jnp.sum(band[d][:, hs], axis=1, keepdims=True), sc)
        for w in lvl_q:
            sc = jnp.where(m["lvl"][w], _mm_nt(lvl_q[w][:, hs], lvl_k[w][:, hs]), sc)
        scs.append(_mm(sc, v[:, hs]))
        yield
    hsl = [slice(h * HD, (h + 1) * HD) for h in range(HEADS)]
    inc = [_mm_tn(v[:, hs], kb[:, hs]) for hs in hsl]
    while states[c] is None:
        yield
    st = states[c]
    states[c + 1] = [st[h] * eb[:, hsl[h]] + inc[h] for h in range(HEADS)]
    yield
    for h in range(HEADS):
        o = scs[h] + _mm_nt(qb[:, hsl[h]], st[h])
        yield
        store(h, _rms(o, norm_w) * jax.nn.silu(p_ref[3, rows, hsl[h]]))
        yield


def _gdn_chunk(p_ref, r0, rows, ps, pst, cw, arow, dtrow, acol, dtcol, states, c, norm_w, store, m):
    hr = range(HEADS)
    hsl = [slice(h * HD, (h + 1) * HD) for h in hr]
    conv = []
    for i in range(3):
        win = p_ref[4 + i, pl.ds(r0, CHUNK + SUB), :]
        acc = win[SUB:SUB + CHUNK, :] * cw[CONV_W - 1:CONV_W, i * GW:(i + 1) * GW]
        for j in range(1, CONV_W):
            acc = acc + win[SUB - j:SUB - j + CHUNK, :] * cw[CONV_W - 1 - j:CONV_W - j, i * GW:(i + 1) * GW]
        conv.append(jax.nn.silu(acc))
        if i == 1:
            qn = [_l2n(conv[0][:, hs]) * (HD ** -0.5) for hs in hsl]
            kn = [_l2n(conv[1][:, hs]) for hs in hsl]
            beta = jax.nn.sigmoid(ps)
            gcol = _cumsum_rows(m["tril"], -jnp.exp(arow) * jax.nn.softplus(ps + dtrow))
            grow = _cumsum_lanes(-jnp.exp(acol) * jax.nn.softplus(pst + dtcol), m["triu"])
            bc = [beta[:, h:h + 1] for h in hr]
            gc = [gcol[:, HEADS + h:HEADS + h + 1] for h in hr]
            lmat = [jnp.where(m["causal"], jnp.exp(gc[h] - grow[HEADS + h:HEADS + h + 1, :]), 0.0) for h in hr]
            kbeta = [kn[h] * bc[h] for h in hr]
            ns = [jnp.where(m["strict"], _mm_nt(kbeta[h], kn[h]) * lmat[h], 0.0) for h in hr]
            tinv_stages = _tri_inv(ns, m)
        yield
    cv = conv[2]
    attn = [jnp.where(m["causal"], _mm_nt(qn[h], kn[h]) * lmat[h], 0.0) for h in hr]
    tinv = yield from tinv_stages
    eg = [jnp.exp(g) for g in gc]
    sol = [_mm(tinv[h], jnp.concatenate([cv[:, hsl[h]] * bc[h], kbeta[h] * eg[h]], axis=1)) for h in hr]
    yield
    kdec = [kn[h] * jnp.exp(gc[h][CHUNK - 1:CHUNK, :] - gc[h]) for h in hr]
    qdec = [qn[h] * eg[h] for h in hr]
    while states[c] is None:
        yield
    s = states[c]
    u = [sol[h][:, :HD] - _mm(sol[h][:, HD:], s[h]) for h in hr]
    yield
    states[c + 1] = [jnp.exp(gc[h][CHUNK - 1:CHUNK, :]) * s[h] + _mm_tn(kdec[h], u[h]) for h in hr]
    yield
    outs = [_mm(qdec[h], s[h]) + _mm(attn[h], u[h]) for h in hr]
    yield
    for h in hr:
        store(h, _rms(outs[h], norm_w) * jax.nn.silu(p_ref[7, rows, hsl[h]]))
        yield


def _gate_norm(o, gate, w):
    parts = []
    for h in range(HEADS):
        hs = slice(h * HD, (h + 1) * HD)
        parts.append(_rms(o[:, hs], w) * jax.nn.silu(gate[:, hs]))
    return jnp.concatenate(parts, axis=1)


def _prompt_body(x_ref, normw_ref, wmain_ref, wsmall_ref, wsmallt_ref, lbl_ref, convw_ref,
                 arow_ref, dtrow_ref, acol_ref, dtcol_ref, hgn_ref, gdnn_ref, fnorm_ref, wout_ref,
                 y_ref, shg_ref, sgdn_ref, sconv_ref,
                 p_ref, ps_ref, hbf_ref, obuf_ref, st_ref, sg_ref):
    ti = pl.program_id(1)
    tb = x_ref.shape[0]

    @pl.when(ti == 0)
    def _():
        st_ref[...] = jnp.zeros_like(st_ref)
        sg_ref[...] = jnp.zeros_like(sg_ref)
        p_ref[4:7, 0:SUB, :] = jnp.zeros((3, SUB, GW), f32)

    x = x_ref[...]
    hb = _rms(x, normw_ref[...]).astype(bf16)
    hbf_ref[...] = hb
    ps_ref[...] = jnp.dot(hb, wsmall_ref[...], preferred_element_type=f32)
    nchunk = tb // CHUNK

    def project():
        done = 0
        for j in range(nchunk):
            hj = hbf_ref[j * CHUNK:(j + 1) * CHUNK, :]
            for g in range(N_GROUPS):
                p_ref[g, SUB + j * CHUNK:SUB + (j + 1) * CHUNK, :] = jnp.dot(
                    hj, wmain_ref[:, g * GW:(g + 1) * GW], preferred_element_type=f32)
                done += 1
                if done % PROJ_RATE == 0:
                    yield

    m = _chunk_masks()
    lbv = _lower_bound(lbl_ref[...])
    cw = convw_ref[...]
    arow, dtrow, acol, dtcol = arow_ref[...], dtrow_ref[...], acol_ref[...], dtcol_ref[...]
    hgn, gdnn = hgn_ref[...], gdnn_ref[...]
    wsmallt = wsmallt_ref[...]

    hg_states = [[st_ref[h] for h in range(HEADS)]] + [None] * nchunk
    gdn_states = [[sg_ref[h] for h in range(HEADS)]] + [None] * nchunk

    def store_at(orows, col0):
        def store(h, val):
            obuf_ref[orows, col0 + h * HD:col0 + (h + 1) * HD] = val.astype(bf16)
        return store

    stages, starts = [project()], [0]
    for j in range(nchunk):
        ready = -(-N_GROUPS * (j + 1) // PROJ_RATE)
        starts += [ready, ready + HGRN_LAG]
        r0 = j * CHUNK
        rows = pl.ds(r0 + SUB, CHUNK)
        orows = pl.ds(r0, CHUNK)
        pst = _mm_nt(wsmallt, hbf_ref[orows, :])[0:2 * HEADS, :]
        stages.append(_gdn_chunk(p_ref, r0, rows, ps_ref[orows, :], pst, cw, arow, dtrow, acol, dtcol,
                                 gdn_states, j, gdnn, store_at(orows, GW), m))
        stages.append(_hgrn_chunk(p_ref, rows, lbv, hg_states, j, hgn, store_at(orows, 0), m))
    _interleave(stages, starts)
    for h in range(HEADS):
        st_ref[h] = hg_states[nchunk][h]
        sg_ref[h] = gdn_states[nchunk][h]

    p_ref[4:7, 0:SUB, :] = p_ref[4:7, tb:tb + SUB, :]

    out = jnp.dot(obuf_ref[...], wout_ref[...], preferred_element_type=f32)
    y_ref[...] = _rms(x + out, fnorm_ref[...])

    @pl.when(ti == pl.num_programs(1) - 1)
    def _():
        for h in range(HEADS):
            shg_ref[h] = st_ref[h].T
            sgdn_ref[h] = sg_ref[h]
        for i in range(3):
            tail = p_ref[4 + i, tb:tb + SUB, :]
            sconv_ref[:, i * GW:(i + 1) * GW] = tail[SUB - (CONV_W - 1):SUB, :]


def _const_spec(shape):
    nd = len(shape)
    return pl.BlockSpec(shape, lambda *_: (0,) * nd)


def _prompt_call(x, params):
    bsz, seq, d = x.shape
    tb = min(TOK_BLOCK, seq)
    assert seq % tb == 0 and tb % CHUNK == 0
    names = ("normw", "wmain", "wsmall", "wsmallt", "lbl", "convw", "arow", "dtrow", "acol", "dtcol",
             "hgn", "gdnn", "fnorm", "wout")
    consts = [params[k] for k in names]
    out_shape = (
        jax.ShapeDtypeStruct((bsz, seq, d), f32),
        jax.ShapeDtypeStruct((bsz, HEADS, HD, HD), f32),
        jax.ShapeDtypeStruct((bsz, HEADS, HD, HD), f32),
        jax.ShapeDtypeStruct((bsz, CONV_W - 1, 3 * GW), f32),
    )
    state_spec = pl.BlockSpec((None, HEADS, HD, HD), lambda b, t: (b, 0, 0, 0))
    return pl.pallas_call(
        _prompt_body,
        grid=(bsz, seq // tb),
        in_specs=[pl.BlockSpec((None, tb, d), lambda b, t: (b, t, 0))] + [_const_spec(c.shape) for c in consts],
        out_specs=(
            pl.BlockSpec((None, tb, d), lambda b, t: (b, t, 0)),
            state_spec, state_spec,
            pl.BlockSpec((None, CONV_W - 1, 3 * GW), lambda b, t: (b, 0, 0)),
        ),
        out_shape=out_shape,
        scratch_shapes=[
            pltpu.VMEM((N_GROUPS, tb + SUB, GW), f32),
            pltpu.VMEM((tb, 128), f32),
            pltpu.VMEM((tb, d), bf16),
            pltpu.VMEM((tb, 2 * GW), bf16),
            pltpu.VMEM((HEADS, HD, HD), f32),
            pltpu.VMEM((HEADS, HD, HD), f32),
        ],
        compiler_params=pltpu.CompilerParams(
            dimension_semantics=("arbitrary", "arbitrary"), vmem_limit_bytes=VMEM_LIMIT),
        name="prompt_mixer",
    )(x, *consts)


def _decode_body(x_ref, normw_ref, wmain_ref, wsmall_ref, lbl_ref, convw_ref, arow_ref, dtrow_ref,
                 hgn_ref, gdnn_ref, fnorm_ref, wout_ref, shg_in, sgdn_in, sconv_in,
                 y_ref, shg_out, sgdn_out, sconv_out,
                 p_ref, ps_ref, o_ref):
    i = pl.program_id(0)
    nb = shg_in.shape[0]

    @pl.when(i == 0)
    def _():
        hb = _rms(x_ref[...], normw_ref[...]).astype(bf16)
        for g in range(N_GROUPS):
            p_ref[g] = jnp.dot(hb, wmain_ref[:, g * GW:(g + 1) * GW], preferred_element_type=f32)
        ps_ref[...] = jnp.dot(hb, wsmall_ref[...], preferred_element_type=f32)

    r0 = pl.multiple_of(i * nb, nb)
    rows = pl.ds(r0, nb)
    lbv = _lower_bound(lbl_ref[...])
    q_hg = p_ref[0, rows, :]
    fg = lbv + (1.0 - lbv) * jax.nn.sigmoid(p_ref[1, rows, :])
    v_hg = p_ref[2, rows, :]

    cw = convw_ref[...]
    cc = 3 * GW
    new = jnp.concatenate([p_ref[4, rows, :], p_ref[5, rows, :], p_ref[6, rows, :]], axis=1)
    acc = new * cw[CONV_W - 1:CONV_W, :]
    for j in range(CONV_W - 1):
        acc = acc + sconv_in[:, j * cc:(j + 1) * cc] * cw[j:j + 1, :]
    for j in range(CONV_W - 2):
        sconv_out[:, j * cc:(j + 1) * cc] = sconv_in[:, (j + 1) * cc:(j + 2) * cc]
    sconv_out[:, (CONV_W - 2) * cc:(CONV_W - 1) * cc] = new
    qkv = jax.nn.silu(acc)
    ps = ps_ref[rows, :]
    beta = jax.nn.sigmoid(ps)
    eg = jnp.exp(-jnp.exp(arow_ref[...]) * jax.nn.softplus(ps + dtrow_ref[...]))

    pairs = [(h, j) for h in range(HEADS) for j in range(nb)]
    hsl = [slice(h * HD, (h + 1) * HD) for h in range(HEADS)]
    ft = [fg[:, hs].T for hs in hsl]
    q_h = [q_hg[:, hs].astype(bf16) for hs in hsl]
    kn = [_l2n(qkv[:, GW + h * HD:GW + (h + 1) * HD]) for h in range(HEADS)]
    knt = [k.T for k in kn]
    kn_b = [k.astype(bf16) for k in kn]
    qn = [(_l2n(qkv[:, hs]) * (HD ** -0.5)).astype(bf16) for hs in hsl]
    for h, j in pairs:
        fcol = ft[h][:, j:j + 1]
        s = fcol * shg_in[j, h] + (1.0 - fcol) * v_hg[j:j + 1, hsl[h]]
        shg_out[j, h] = s
        o_ref[i, j:j + 1, hsl[h]] = _mm(q_h[h], s)[j:j + 1, :]
    ks = {}
    for h, j in pairs:
        s = eg[j:j + 1, HEADS + h:HEADS + h + 1] * sgdn_in[j, h]
        sgdn_out[j, h] = s
        ks[h, j] = _mm(kn_b[h], s)[j:j + 1, :]
    for h, j in pairs:
        v_row = qkv[j:j + 1, 2 * GW + h * HD:2 * GW + (h + 1) * HD]
        delta = (v_row - ks[h, j]) * beta[j:j + 1, h:h + 1]
        s = sgdn_out[j, h] + knt[h][:, j:j + 1] * delta
        sgdn_out[j, h] = s
        o_ref[i, j:j + 1, GW + h * HD:GW + (h + 1) * HD] = _mm(qn[h], s)[j:j + 1, :]

    @pl.when(i == pl.num_programs(0) - 1)
    def _():
        o = o_ref[...].reshape(x_ref.shape[0], 2 * GW)
        og = jnp.concatenate([_gate_norm(o[:, :GW], p_ref[3], hgn_ref[...]),
                              _gate_norm(o[:, GW:], p_ref[7], gdnn_ref[...])], axis=1)
        out = jnp.dot(og.astype(bf16), wout_ref[...], preferred_element_type=f32)
        y_ref[...] = _rms(x_ref[...] + out, fnorm_ref[...])


def _decode_call(x, shg, sgdn, sconv, params):
    n, d = x.shape
    nb = min(DEC_BLOCK, n)
    assert n % nb == 0
    names = ("normw", "wmain", "wsmall", "lbl", "convw", "arow", "dtrow", "hgn", "gdnn", "fnorm", "wout")
    consts = [params[k] for k in names]
    state_spec = pl.BlockSpec((nb, HEADS, HD, HD), lambda i: (i, 0, 0, 0))
    conv_spec = pl.BlockSpec((nb, sconv.shape[1]), lambda i: (i, 0))
    return pl.pallas_call(
        _decode_body,
        grid=(n // nb,),
        in_specs=[_const_spec(x.shape)] + [_const_spec(c.shape) for c in consts]
        + [state_spec, state_spec, conv_spec],
        out_specs=(_const_spec(x.shape), state_spec, state_spec, conv_spec),
        out_shape=(
            jax.ShapeDtypeStruct((n, d), f32),
            jax.ShapeDtypeStruct(shg.shape, f32),
            jax.ShapeDtypeStruct(sgdn.shape, f32),
            jax.ShapeDtypeStruct(sconv.shape, f32),
        ),
        scratch_shapes=[
            pltpu.VMEM((N_GROUPS, n, GW), f32),
            pltpu.VMEM((n, 128), f32),
            pltpu.VMEM((n // nb, nb, 2 * GW), f32),
        ],
        compiler_params=pltpu.CompilerParams(
            dimension_semantics=("arbitrary",), vmem_limit_bytes=VMEM_LIMIT),
        name="decode_step",
    )(x, *consts, shg, sgdn, sconv)


def _prepare(norm_w, w_in, hg_lb_logits, conv_w, gdn_a_log, gdn_dt_bias, hg_out_norm, gdn_out_norm,
             w_out, final_norm):
    d = w_in.shape[1]
    w = w_in[0]
    nmain = N_GROUPS * GW
    small = w[:, nmain:]
    zrow = jnp.zeros((1, HEADS), f32)
    a = gdn_a_log[0].reshape(1, HEADS).astype(f32)
    dt = gdn_dt_bias[0].reshape(1, HEADS).astype(f32)
    return dict(
        normw=norm_w[0].reshape(1, d),
        wmain=w.astype(bf16),
        wsmall=jnp.pad(small, ((0, 0), (0, 128 - 2 * HEADS))).astype(bf16),
        wsmallt=jnp.pad(small.T, ((0, 16 - 2 * HEADS), (0, 0))).astype(bf16),
        lbl=hg_lb_logits.astype(f32),
        convw=conv_w[0].astype(f32),
        arow=jnp.pad(jnp.concatenate([zrow, a], axis=1), ((0, 0), (0, 128 - 2 * HEADS))),
        dtrow=jnp.pad(jnp.concatenate([zrow, dt], axis=1), ((0, 0), (0, 128 - 2 * HEADS))),
        acol=jnp.concatenate([zrow, a], axis=1).reshape(2 * HEADS, 1),
        dtcol=jnp.concatenate([zrow, dt], axis=1).reshape(2 * HEADS, 1),
        hgn=hg_out_norm[0].reshape(1, HD),
        gdnn=gdn_out_norm[0].reshape(1, HD),
        fnorm=final_norm.reshape(1, d),
        wout=w_out[0].astype(bf16),
    )


def kernel(x_prompt, x_sample, state_hgrn, state_gdn, state_gdn_conv, norm_w, w_in, hg_lb_logits, conv_w, gdn_a_log, gdn_dt_bias, hg_out_norm, gdn_out_norm, w_out, final_norm):
    assert w_in.shape[0] == 1, "single-layer kernel"
    params = _prepare(norm_w, w_in, hg_lb_logits, conv_w, gdn_a_log, gdn_dt_bias, hg_out_norm,
                      gdn_out_norm, w_out, final_norm)
    y_p, hg_p, gdn_p, conv_p = _prompt_call(x_prompt, params)
    n, dec_seq, d = x_sample.shape
    assert dec_seq == 1, "decode path is a single-token step"
    conv_shape = state_gdn_conv.shape[1:]
    y_s, hg_s, gdn_s, conv_s = _decode_call(x_sample.reshape(n, d), state_hgrn[0], state_gdn[0],
                                            state_gdn_conv[0].reshape(n, -1), params)
    return (y_p, y_s.reshape(n, 1, d), hg_p[None], gdn_p[None], conv_p[None],
            hg_s[None], gdn_s[None], conv_s.reshape(conv_shape)[None])
```

```python
import functools

import jax
import jax.numpy as jnp
from jax import lax
from jax.experimental import pallas as pl
from jax.experimental.pallas import tpu as pltpu

f32 = jnp.float32
bf16 = jnp.bfloat16

HEADS = 4
HD = 128
GW = HEADS * HD
N_GROUPS = 8
CONV_W = 4
CHUNK = 128
SUB = 8
EPS = 1e-6
TOK_BLOCK = 512
DEC_BLOCK = 8
PROJ_RATE = 4
HGRN_LAG = 1
VMEM_LIMIT = 56 * 1024 * 1024


def _mm(a, b):
    return jnp.dot(a.astype(bf16), b.astype(bf16), preferred_element_type=f32)


def _mm_nt(a, b):
    return lax.dot_general(a.astype(bf16), b.astype(bf16), (((1,), (1,)), ((), ())),
                           preferred_element_type=f32)


def _mm_tn(a, b):
    return lax.dot_general(a.astype(bf16), b.astype(bf16), (((0,), (0,)), ((), ())),
                           preferred_element_type=f32)


def _split3(x):
    a1 = x.astype(bf16)
    r1 = x - a1.astype(f32)
    a2 = r1.astype(bf16)
    r2 = r1 - a2.astype(f32)
    return a1, a2, r2.astype(bf16)


def _cumsum_rows(tril, x):
    a1, a2, a3 = _split3(x)
    d = lambda a: jnp.dot(tril, a, preferred_element_type=f32)
    return d(a1) + d(a2) + d(a3)


def _cumsum_lanes(x, triu):
    a1, a2, a3 = _split3(x)
    d = lambda a: jnp.dot(a, triu, preferred_element_type=f32)
    return d(a1) + d(a2) + d(a3)


def _rms(x, w):
    return x * lax.rsqrt(jnp.mean(x * x, axis=-1, keepdims=True) + EPS) * w


def _l2n(x):
    return x * lax.rsqrt(jnp.sum(x * x, axis=-1, keepdims=True) + EPS)


def _lower_bound(lbl):
    m = jnp.max(lbl, axis=0, keepdims=True)
    e = jnp.exp(lbl - m)
    return e[0:1, :] / jnp.sum(e, axis=0, keepdims=True)


def _chunk_masks():
    t = lax.broadcasted_iota(jnp.int32, (CHUNK, CHUNK), 0)
    s = lax.broadcasted_iota(jnp.int32, (CHUNK, CHUNK), 1)
    m = {}
    m["causal"] = s <= t
    m["strict"] = s < t
    m["eye"] = jnp.where(s == t, 1.0, 0.0).astype(f32)
    m["tril"] = jnp.where(s <= t, 1.0, 0.0).astype(bf16)
    m["triu"] = jnp.where(t <= s, 1.0, 0.0).astype(bf16)
    m["blk"] = (t // SUB) == (s // SUB)
    m["band"] = [(s == t - d) & ((t % SUB) >= d) for d in range(SUB)]
    lv = {}
    w = SUB
    while w < CHUNK:
        lv[w] = ((t // (2 * w)) == (s // (2 * w))) & (((t // w) % 2) == 1) & (((s // w) % 2) == 0)
        w *= 2
    m["lvl"] = lv
    return m


def _odd_rows(x, w):
    return jnp.concatenate([x[i * w:(i + 1) * w] for i in range(1, CHUNK // w, 2)], axis=0)


def _with_odd_rows(x, odd, w):
    parts = []
    for i in range(CHUNK // w):
        parts.append(odd[(i // 2) * w:(i // 2 + 1) * w] if i % 2 else x[i * w:(i + 1) * w])
    return jnp.concatenate(parts, axis=0)


def _tri_inv(ns, m):
    eye = m["eye"]
    ps = [jnp.where(m["blk"], -n, 0.0) for n in ns]
    ts = [eye + p for p in ps]
    w = 2
    while w < SUB:
        ps = [_mm(p, p) for p in ps]
        yield
        ts = [_mm(t, eye + p) for t, p in zip(ts, ps)]
        yield
        w *= 2
    w = SUB
    while w < CHUNK:
        zero = jnp.zeros_like(ts[0])
        mo = _odd_rows(m["lvl"][w], w)
        tbs = [t.astype(bf16) for t in ts]
        fts = [_mm(jnp.where(mo, _odd_rows(n, w), 0.0), tb) for n, tb in zip(ns, tbs)]
        yield
        ts = [_with_odd_rows(t, _odd_rows(t, w) - _mm(_odd_rows(tb, w), _with_odd_rows(zero, ft, w)), w)
              for t, tb, ft in zip(ts, tbs, fts)]
        yield
        w *= 2
    return ts


def _interleave(stages, starts):
    pending = sorted(zip(starts, range(len(stages))))
    live, turn = [], 0
    while live or pending:
        while pending and pending[0][0] <= turn:
            live.append(stages[pending.pop(0)[1]])
        for g in list(live):
            try:
                next(g)
            except StopIteration:
                live.remove(g)
        turn += 1


def _hgrn_chunk(p_ref, rows, lbv, states, c, norm_w, store, m):
    q, hf, v = p_ref[0, rows, :], p_ref[1, rows, :], p_ref[2, rows, :]
    fg = lbv + (1.0 - lbv) * jax.nn.sigmoid(hf)
    kk = 1.0 - fg
    b = _cumsum_rows(m["tril"], jnp.log(fg))
    yield
    nsub = CHUNK // SUB
    b3 = b.reshape(nsub, SUB, GW)
    q3 = q.reshape(nsub, SUB, GW)
    k3 = kk.reshape(nsub, SUB, GW)
    last = [b3[j, SUB - 1:SUB, :] for j in range(nsub)]
    blast = last[nsub - 1]
    zero = jnp.zeros((SUB, GW), f32)

    lvl_q, lvl_k = {}, {}
    w = SUB
    while w < CHUNK:
        per = w // SUB
        qs, ks = [], []
        for j in range(nsub):
            blk = j // per
            if blk % 2 == 1:
                qs.append(q3[j] * jnp.exp(b3[j] - last[blk * per - 1]))
                ks.append(zero)
            else:
                ks.append(k3[j] * jnp.exp(last[blk * per + per - 1] - b3[j]))
        lvl_q[w] = jnp.concatenate(qs, axis=0)
        lvl_k[w] = jnp.concatenate(ks, axis=0)
        yield
        w *= 2

    fg3 = fg.reshape(nsub, SUB, GW)
    z = k3
    band = [q * kk]
    for d in range(1, SUB):
        z = fg3 * pltpu.roll(z, 1, 1)
        band.append((q3 * z).reshape(CHUNK, GW))
        yield

    qb = q * jnp.exp(b)
    kb = kk * jnp.exp(blast - b)
    eb = jnp.exp(blast)
    yield
    scs = []
    for h in range(HEADS):
        hs = slice(h * HD, (h + 1) * HD)
        sc = 0.0
        for d in range(SUB):
            sc = jnp.where(m["band"][d], jnp.sum(band[d][:, hs], axis=1, keepdims=True), sc)
        for w in lvl_q:
            pw = _mm_nt(lvl_q[w][:, hs], lvl_k[w][:, hs])
            sc = _with_odd_rows(sc, jnp.where(_odd_rows(m["lvl"][w], w), pw, _odd_rows(sc, w)), w)
        scs.append(_mm(sc, v[:, hs]))
        yield
    hsl = [slice(h * HD, (h + 1) * HD) for h in range(HEADS)]
    inc = [_mm_tn(v[:, hs], kb[:, hs]) for hs in hsl]
    while states[c] is None:
        yield
    st = states[c]
    states[c + 1] = [st[h] * eb[:, hsl[h]] + inc[h] for h in range(HEADS)]
    yield
    for h in range(HEADS):
        o = scs[h] + _mm_nt(qb[:, hsl[h]], st[h])
        yield
        store(h, _rms(o, norm_w) * jax.nn.silu(p_ref[3, rows, hsl[h]]))
        yield


def _gdn_chunk(p_ref, r0, rows, ps, pst, cw, arow, dtrow, acol, dtcol, states, c, norm_w, store, m):
    hr = range(HEADS)
    hsl = [slice(h * HD, (h + 1) * HD) for h in hr]
    conv = []
    for i in range(3):
        win = p_ref[4 + i, pl.ds(r0, CHUNK + SUB), :]
        acc = win[SUB:SUB + CHUNK, :] * cw[CONV_W - 1:CONV_W, i * GW:(i + 1) * GW]
        for j in range(1, CONV_W):
            acc = acc + win[SUB - j:SUB - j + CHUNK, :] * cw[CONV_W - 1 - j:CONV_W - j, i * GW:(i + 1) * GW]
        conv.append(jax.nn.silu(acc))
        if i == 1:
            qn = [_l2n(conv[0][:, hs]) * (HD ** -0.5) for hs in hsl]
            kn = [_l2n(conv[1][:, hs]) for hs in hsl]
            beta = jax.nn.sigmoid(ps)
            gcol = _cumsum_rows(m["tril"], -jnp.exp(arow) * jax.nn.softplus(ps + dtrow))
            grow = _cumsum_lanes(-jnp.exp(acol) * jax.nn.softplus(pst + dtcol), m["triu"])
            bc = [beta[:, h:h + 1] for h in hr]
            gc = [gcol[:, HEADS + h:HEADS + h + 1] for h in hr]
            lmat = [jnp.where(m["causal"], jnp.exp(gc[h] - grow[HEADS + h:HEADS + h + 1, :]), 0.0) for h in hr]
            kbeta = [kn[h] * bc[h] for h in hr]
            ns = [jnp.where(m["strict"], _mm_nt(kbeta[h], kn[h]) * lmat[h], 0.0) for h in hr]
            tinv_stages = _tri_inv(ns, m)
        yield
    cv = conv[2]
    attn = [jnp.where(m["causal"], _mm_nt(qn[h], kn[h]) * lmat[h], 0.0) for h in hr]
    tinv = yield from tinv_stages
    eg = [jnp.exp(g) for g in gc]
    sol = [_mm(tinv[h], jnp.concatenate([cv[:, hsl[h]] * bc[h], kbeta[h] * eg[h]], axis=1)) for h in hr]
    yield
    kdec = [kn[h] * jnp.exp(gc[h][CHUNK - 1:CHUNK, :] - gc[h]) for h in hr]
    qdec = [qn[h] * eg[h] for h in hr]
    while states[c] is None:
        yield
    s = states[c]
    u = [sol[h][:, :HD] - _mm(sol[h][:, HD:], s[h]) for h in hr]
    yield
    states[c + 1] = [jnp.exp(gc[h][CHUNK - 1:CHUNK, :]) * s[h] + _mm_tn(kdec[h], u[h]) for h in hr]
    yield
    outs = [_mm(qdec[h], s[h]) + _mm(attn[h], u[h]) for h in hr]
    yield
    for h in hr:
        store(h, _rms(outs[h], norm_w) * jax.nn.silu(p_ref[7, rows, hsl[h]]))
        yield


def _gate_norm(o, gate, w):
    parts = []
    for h in range(HEADS):
        hs = slice(h * HD, (h + 1) * HD)
        parts.append(_rms(o[:, hs], w) * jax.nn.silu(gate[:, hs]))
    return jnp.concatenate(parts, axis=1)


def _prompt_body(x_ref, normw_ref, wmain_ref, wsmall_ref, wsmallt_ref, lbl_ref, convw_ref,
                 arow_ref, dtrow_ref, acol_ref, dtcol_ref, hgn_ref, gdnn_ref, fnorm_ref, wout_ref,
                 y_ref, shg_ref, sgdn_ref, sconv_ref,
                 p_ref, ps_ref, hbf_ref, obuf_ref, st_ref, sg_ref):
    ti = pl.program_id(1)
    tb = x_ref.shape[0]

    @pl.when(ti == 0)
    def _():
        st_ref[...] = jnp.zeros_like(st_ref)
        sg_ref[...] = jnp.zeros_like(sg_ref)
        p_ref[4:7, 0:SUB, :] = jnp.zeros((3, SUB, GW), f32)

    x = x_ref[...]
    hb = _rms(x, normw_ref[...]).astype(bf16)
    hbf_ref[...] = hb
    ps_ref[...] = jnp.dot(hb, wsmall_ref[...], preferred_element_type=f32)
    nchunk = tb // CHUNK

    def project():
        done = 0
        for j in range(nchunk):
            hj = hbf_ref[j * CHUNK:(j + 1) * CHUNK, :]
            for g in range(N_GROUPS):
                p_ref[g, SUB + j * CHUNK:SUB + (j + 1) * CHUNK, :] = jnp.dot(
                    hj, wmain_ref[:, g * GW:(g + 1) * GW], preferred_element_type=f32)
                done += 1
                if done % PROJ_RATE == 0:
                    yield

    m = _chunk_masks()
    lbv = _lower_bound(lbl_ref[...])
    cw = convw_ref[...]
    arow, dtrow, acol, dtcol = arow_ref[...], dtrow_ref[...], acol_ref[...], dtcol_ref[...]
    hgn, gdnn = hgn_ref[...], gdnn_ref[...]
    wsmallt = wsmallt_ref[...]

    hg_states = [[st_ref[h] for h in range(HEADS)]] + [None] * nchunk
    gdn_states = [[sg_ref[h] for h in range(HEADS)]] + [None] * nchunk

    def store_at(orows, col0):
        def store(h, val):
            obuf_ref[orows, col0 + h * HD:col0 + (h + 1) * HD] = val.astype(bf16)
        return store

    stages, starts = [project()], [0]
    for j in range(nchunk):
        ready = -(-N_GROUPS * (j + 1) // PROJ_RATE)
        starts += [ready, ready + HGRN_LAG]
        r0 = j * CHUNK
        rows = pl.ds(r0 + SUB, CHUNK)
        orows = pl.ds(r0, CHUNK)
        pst = _mm_nt(wsmallt, hbf_ref[orows, :])[0:2 * HEADS, :]
        stages.append(_gdn_chunk(p_ref, r0, rows, ps_ref[orows, :], pst, cw, arow, dtrow, acol, dtcol,
                                 gdn_states, j, gdnn, store_at(orows, GW), m))
        stages.append(_hgrn_chunk(p_ref, rows, lbv, hg_states, j, hgn, store_at(orows, 0), m))
    _interleave(stages, starts)
    for h in range(HEADS):
        st_ref[h] = hg_states[nchunk][h]
        sg_ref[h] = gdn_states[nchunk][h]

    p_ref[4:7, 0:SUB, :] = p_ref[4:7, tb:tb + SUB, :]

    out = jnp.dot(obuf_ref[...], wout_ref[...], preferred_element_type=f32)
    y_ref[...] = _rms(x + out, fnorm_ref[...])

    @pl.when(ti == pl.num_programs(1) - 1)
    def _():
        for h in range(HEADS):
            shg_ref[h] = st_ref[h].T
            sgdn_ref[h] = sg_ref[h]
        for i in range(3):
            tail = p_ref[4 + i, tb:tb + SUB, :]
            sconv_ref[:, i * GW:(i + 1) * GW] = tail[SUB - (CONV_W - 1):SUB, :]


def _const_spec(shape):
    nd = len(shape)
    return pl.BlockSpec(shape, lambda *_: (0,) * nd)


def _prompt_call(x, params):
    bsz, seq, d = x.shape
    tb = min(TOK_BLOCK, seq)
    assert seq % tb == 0 and tb % CHUNK == 0
    names = ("normw", "wmain", "wsmall", "wsmallt", "lbl", "convw", "arow", "dtrow", "acol", "dtcol",
             "hgn", "gdnn", "fnorm", "wout")
    consts = [params[k] for k in names]
    out_shape = (
        jax.ShapeDtypeStruct((bsz, seq, d), f32),
        jax.ShapeDtypeStruct((bsz, HEADS, HD, HD), f32),
        jax.ShapeDtypeStruct((bsz, HEADS, HD, HD), f32),
        jax.ShapeDtypeStruct((bsz, CONV_W - 1, 3 * GW), f32),
    )
    state_spec = pl.BlockSpec((None, HEADS, HD, HD), lambda b, t: (b, 0, 0, 0))
    return pl.pallas_call(
        _prompt_body,
        grid=(bsz, seq // tb),
        in_specs=[pl.BlockSpec((None, tb, d), lambda b, t: (b, t, 0))] + [_const_spec(c.shape) for c in consts],
        out_specs=(
            pl.BlockSpec((None, tb, d), lambda b, t: (b, t, 0)),
            state_spec, state_spec,
            pl.BlockSpec((None, CONV_W - 1, 3 * GW), lambda b, t: (b, 0, 0)),
        ),
        out_shape=out_shape,
        scratch_shapes=[
            pltpu.VMEM((N_GROUPS, tb + SUB, GW), f32),
            pltpu.VMEM((tb, 128), f32),
            pltpu.VMEM((tb, d), bf16),
            pltpu.VMEM((tb, 2 * GW), bf16),
            pltpu.VMEM((HEADS, HD, HD), f32),
            pltpu.VMEM((HEADS, HD, HD), f32),
        ],
        compiler_params=pltpu.CompilerParams(
            dimension_semantics=("arbitrary", "arbitrary"), vmem_limit_bytes=VMEM_LIMIT),
        name="prompt_mixer",
    )(x, *consts)


def _decode_body(x_ref, normw_ref, wmain_ref, wsmall_ref, lbl_ref, convw_ref, arow_ref, dtrow_ref,
                 hgn_ref, gdnn_ref, fnorm_ref, wout_ref, shg_in, sgdn_in, sconv_in,
                 y_ref, shg_out, sgdn_out, sconv_out,
                 p_ref, ps_ref, o_ref):
    i = pl.program_id(0)
    nb = shg_in.shape[0]

    @pl.when(i == 0)
    def _():
        hb = _rms(x_ref[...], normw_ref[...]).astype(bf16)
        for g in range(N_GROUPS):
            p_ref[g] = jnp.dot(hb, wmain_ref[:, g * GW:(g + 1) * GW], preferred_element_type=f32)
        ps_ref[...] = jnp.dot(hb, wsmall_ref[...], preferred_element_type=f32)

    r0 = pl.multiple_of(i * nb, nb)
    rows = pl.ds(r0, nb)
    lbv = _lower_bound(lbl_ref[...])
    q_hg = p_ref[0, rows, :]
    fg = lbv + (1.0 - lbv) * jax.nn.sigmoid(p_ref[1, rows, :])
    v_hg = p_ref[2, rows, :]

    cw = convw_ref[...]
    cc = 3 * GW
    new = jnp.concatenate([p_ref[4, rows, :], p_ref[5, rows, :], p_ref[6, rows, :]], axis=1)
    acc = new * cw[CONV_W - 1:CONV_W, :]
    for j in range(CONV_W - 1):
        acc = acc + sconv_in[:, j * cc:(j + 1) * cc] * cw[j:j + 1, :]
    for j in range(CONV_W - 2):
        sconv_out[:, j * cc:(j + 1) * cc] = sconv_in[:, (j + 1) * cc:(j + 2) * cc]
    sconv_out[:, (CONV_W - 2) * cc:(CONV_W - 1) * cc] = new
    qkv = jax.nn.silu(acc)
    ps = ps_ref[rows, :]
    beta = jax.nn.sigmoid(ps)
    eg = jnp.exp(-jnp.exp(arow_ref[...]) * jax.nn.softplus(ps + dtrow_ref[...]))

    pairs = [(h, j) for h in range(HEADS) for j in range(nb)]
    hsl = [slice(h * HD, (h + 1) * HD) for h in range(HEADS)]
    ft = [fg[:, hs].T for hs in hsl]
    q_h = [q_hg[:, hs].astype(bf16) for hs in hsl]
    kn = [_l2n(qkv[:, GW + h * HD:GW + (h + 1) * HD]) for h in range(HEADS)]
    knt = [k.T for k in kn]
    kn_b = [k.astype(bf16) for k in kn]
    qn = [(_l2n(qkv[:, hs]) * (HD ** -0.5)).astype(bf16) for hs in hsl]
    for h, j in pairs:
        fcol = ft[h][:, j:j + 1]
        s = fcol * shg_in[j, h] + (1.0 - fcol) * v_hg[j:j + 1, hsl[h]]
        shg_out[j, h] = s
        o_ref[i, j:j + 1, hsl[h]] = _mm(q_h[h], s)[j:j + 1, :]
    ks = {}
    for h, j in pairs:
        s = eg[j:j + 1, HEADS + h:HEADS + h + 1] * sgdn_in[j, h]
        sgdn_out[j, h] = s
        ks[h, j] = _mm(kn_b[h], s)[j:j + 1, :]
    for h, j in pairs:
        v_row = qkv[j:j + 1, 2 * GW + h * HD:2 * GW + (h + 1) * HD]
        delta = (v_row - ks[h, j]) * beta[j:j + 1, h:h + 1]
        s = sgdn_out[j, h] + knt[h][:, j:j + 1] * delta
        sgdn_out[j, h] = s
        o_ref[i, j:j + 1, GW + h * HD:GW + (h + 1) * HD] = _mm(qn[h], s)[j:j + 1, :]

    @pl.when(i == pl.num_programs(0) - 1)
    def _():
        o = o_ref[...].reshape(x_ref.shape[0], 2 * GW)
        og = jnp.concatenate([_gate_norm(o[:, :GW], p_ref[3], hgn_ref[...]),
                              _gate_norm(o[:, GW:], p_ref[7], gdnn_ref[...])], axis=1)
        out = jnp.dot(og.astype(bf16), wout_ref[...], preferred_element_type=f32)
        y_ref[...] = _rms(x_ref[...] + out, fnorm_ref[...])


def _decode_call(x, shg, sgdn, sconv, params):
    n, d = x.shape
    nb = min(DEC_BLOCK, n)
    assert n % nb == 0
    names = ("normw", "wmain", "wsmall", "lbl", "convw", "arow", "dtrow", "hgn", "gdnn", "fnorm", "wout")
    consts = [params[k] for k in names]
    state_spec = pl.BlockSpec((nb, HEADS, HD, HD), lambda i: (i, 0, 0, 0))
    conv_spec = pl.BlockSpec((nb, sconv.shape[1]), lambda i: (i, 0))
    return pl.pallas_call(
        _decode_body,
        grid=(n // nb,),
        in_specs=[_const_spec(x.shape)] + [_const_spec(c.shape) for c in consts]
        + [state_spec, state_spec, conv_spec],
        out_specs=(_const_spec(x.shape), state_spec, state_spec, conv_spec),
        out_shape=(
            jax.ShapeDtypeStruct((n, d), f32),
            jax.ShapeDtypeStruct(shg.shape, f32),
            jax.ShapeDtypeStruct(sgdn.shape, f32),
            jax.ShapeDtypeStruct(sconv.shape, f32),
        ),
        scratch_shapes=[
            pltpu.VMEM((N_GROUPS, n, GW), f32),
            pltpu.VMEM((n, 128), f32),
            pltpu.VMEM((n // nb, nb, 2 * GW), f32),
        ],
        compiler_params=pltpu.CompilerParams(
            dimension_semantics=("arbitrary",), vmem_limit_bytes=VMEM_LIMIT),
        name="decode_step",
    )(x, *consts, shg, sgdn, sconv)


def _prepare(norm_w, w_in, hg_lb_logits, conv_w, gdn_a_log, gdn_dt_bias, hg_out_norm, gdn_out_norm,
             w_out, final_norm):
    d = w_in.shape[1]
    w = w_in[0]
    nmain = N_GROUPS * GW
    small = w[:, nmain:]
    zrow = jnp.zeros((1, HEADS), f32)
    a = gdn_a_log[0].reshape(1, HEADS).astype(f32)
    dt = gdn_dt_bias[0].reshape(1, HEADS).astype(f32)
    return dict(
        normw=norm_w[0].reshape(1, d),
        wmain=w.astype(bf16),
        wsmall=jnp.pad(small, ((0, 0), (0, 128 - 2 * HEADS))).astype(bf16),
        wsmallt=jnp.pad(small.T, ((0, 16 - 2 * HEADS), (0, 0))).astype(bf16),
        lbl=hg_lb_logits.astype(f32),
        convw=conv_w[0].astype(f32),
        arow=jnp.pad(jnp.concatenate([zrow, a], axis=1), ((0, 0), (0, 128 - 2 * HEADS))),
        dtrow=jnp.pad(jnp.concatenate([zrow, dt], axis=1), ((0, 0), (0, 128 - 2 * HEADS))),
        acol=jnp.concatenate([zrow, a], axis=1).reshape(2 * HEADS, 1),
        dtcol=jnp.concatenate([zrow, dt], axis=1).reshape(2 * HEADS, 1),
        hgn=hg_out_norm[0].reshape(1, HD),
        gdnn=gdn_out_norm[0].reshape(1, HD),
        fnorm=final_norm.reshape(1, d),
        wout=w_out[0].astype(bf16),
    )


def kernel(x_prompt, x_sample, state_hgrn, state_gdn, state_gdn_conv, norm_w, w_in, hg_lb_logits, conv_w, gdn_a_log, gdn_dt_bias, hg_out_norm, gdn_out_norm, w_out, final_norm):
    assert w_in.shape[0] == 1, "single-layer kernel"
    params = _prepare(norm_w, w_in, hg_lb_logits, conv_w, gdn_a_log, gdn_dt_bias, hg_out_norm,
                      gdn_out_norm, w_out, final_norm)
    y_p, hg_p, gdn_p, conv_p = _prompt_call(x_prompt, params)
    n, dec_seq, d = x_sample.shape
    assert dec_seq == 1, "decode path is a single-token step"
    conv_shape = state_gdn_conv.shape[1:]
    y_s, hg_s, gdn_s, conv_s = _decode_call(x_sample.reshape(n, d), state_hgrn[0], state_gdn[0],
                                            state_gdn_conv[0].reshape(n, -1), params)
    return (y_p, y_s.reshape(n, 1, d), hg_p[None], gdn_p[None], conv_p[None],
            hg_s[None], gdn_s[None], conv_s.reshape(conv_shape)[None])
```

```python
import jax
import jax.numpy as jnp
from jax import lax
from jax.experimental import pallas as pl
from jax.experimental.pallas import tpu as pltpu

f32 = jnp.float32
bf16 = jnp.bfloat16

HEADS = 4
HD = 128
GW = HEADS * HD
N_GROUPS = 8
CONV_W = 4
CHUNK = 128
SUB = 8
EPS = 1e-6
TOK_BLOCK = 512
DEC_BLOCK = 16
PROJ_RATE = 4
HGRN_LAG = 1
VMEM_LIMIT = 56 * 1024 * 1024


def _mm(a, b):
    return jnp.dot(a.astype(bf16), b.astype(bf16), preferred_element_type=f32)


def _mm_nt(a, b):
    return lax.dot_general(a.astype(bf16), b.astype(bf16), (((1,), (1,)), ((), ())),
                           preferred_element_type=f32)


def _mm_tn(a, b):
    return lax.dot_general(a.astype(bf16), b.astype(bf16), (((0,), (0,)), ((), ())),
                           preferred_element_type=f32)


def _split3(x):
    a1 = x.astype(bf16)
    r1 = x - a1.astype(f32)
    a2 = r1.astype(bf16)
    r2 = r1 - a2.astype(f32)
    return a1, a2, r2.astype(bf16)


def _cumsum_rows(tril, x):
    a1, a2, a3 = _split3(x)
    d = lambda a: jnp.dot(tril, a, preferred_element_type=f32)
    return d(a1) + d(a2) + d(a3)


def _cumsum_lanes(x, triu):
    a1, a2, a3 = _split3(x)
    d = lambda a: jnp.dot(a, triu, preferred_element_type=f32)
    return d(a1) + d(a2) + d(a3)


def _rms(x, w):
    return x * lax.rsqrt(jnp.mean(x * x, axis=-1, keepdims=True) + EPS) * w


def _l2n(x):
    return x * lax.rsqrt(jnp.sum(x * x, axis=-1, keepdims=True) + EPS)


def _lower_bound(lbl):
    m = jnp.max(lbl, axis=0, keepdims=True)
    e = jnp.exp(lbl - m)
    return e[0:1, :] / jnp.sum(e, axis=0, keepdims=True)


def _chunk_masks():
    t = lax.broadcasted_iota(jnp.int32, (CHUNK, CHUNK), 0)
    s = lax.broadcasted_iota(jnp.int32, (CHUNK, CHUNK), 1)
    m = {}
    m["causal"] = s <= t
    m["strict"] = s < t
    m["eye"] = jnp.where(s == t, 1.0, 0.0).astype(f32)
    m["tril"] = jnp.where(s <= t, 1.0, 0.0).astype(bf16)
    m["triu"] = jnp.where(t <= s, 1.0, 0.0).astype(bf16)
    m["blk"] = (t // SUB) == (s // SUB)
    m["band"] = [(s == t - d) & ((t % SUB) >= d) for d in range(SUB)]
    lv = {}
    w = SUB
    while w < CHUNK:
        lv[w] = ((t // (2 * w)) == (s // (2 * w))) & (((t // w) % 2) == 1) & (((s // w) % 2) == 0)
        w *= 2
    m["lvl"] = lv
    return m


def _odd_rows(x, w):
    return jnp.concatenate([x[i * w:(i + 1) * w] for i in range(1, CHUNK // w, 2)], axis=0)


def _with_odd_rows(x, odd, w):
    parts = []
    for i in range(CHUNK // w):
        parts.append(odd[(i // 2) * w:(i // 2 + 1) * w] if i % 2 else x[i * w:(i + 1) * w])
    return jnp.concatenate(parts, axis=0)


def _tri_inv(ns, m):
    eye = m["eye"]
    ps = [jnp.where(m["blk"], -n, 0.0) for n in ns]
    ts = [eye + p for p in ps]
    w = 2
    while w < SUB:
        ps = [_mm(p, p) for p in ps]
        yield
        ts = [_mm(t, eye + p) for t, p in zip(ts, ps)]
        yield
        w *= 2
    w = SUB
    while w < CHUNK:
        zero = jnp.zeros_like(ts[0])
        mo = _odd_rows(m["lvl"][w], w)
        tbs = [t.astype(bf16) for t in ts]
        fts = [_mm(jnp.where(mo, _odd_rows(n, w), 0.0), tb) for n, tb in zip(ns, tbs)]
        yield
        ts = [_with_odd_rows(t, _odd_rows(t, w) - _mm(_odd_rows(tb, w), _with_odd_rows(zero, ft, w)), w)
              for t, tb, ft in zip(ts, tbs, fts)]
        yield
        w *= 2
    return ts


def _interleave(stages, starts):
    pending = sorted(zip(starts, range(len(stages))))
    live, turn = [], 0
    while live or pending:
        while pending and pending[0][0] <= turn:
            live.append(stages[pending.pop(0)[1]])
        for g in list(live):
            try:
                next(g)
            except StopIteration:
                live.remove(g)
        turn += 1


def _hgrn_chunk(p_ref, rows, lbv, states, c, norm_w, store, m):
    q, hf, v = p_ref[0, rows, :], p_ref[1, rows, :], p_ref[2, rows, :]
    fg = lbv + (1.0 - lbv) * jax.nn.sigmoid(hf)
    kk = 1.0 - fg
    b = _cumsum_rows(m["tril"], jnp.log(fg))
    yield
    nsub = CHUNK // SUB
    b3 = b.reshape(nsub, SUB, GW)
    q3 = q.reshape(nsub, SUB, GW)
    k3 = kk.reshape(nsub, SUB, GW)
    last = [b3[j, SUB - 1:SUB, :] for j in range(nsub)]
    blast = last[nsub - 1]
    zero = jnp.zeros((SUB, GW), f32)

    lvl_q, lvl_k = {}, {}
    w = SUB
    while w < CHUNK:
        per = w // SUB
        qs, ks = [], []
        for j in range(nsub):
            blk = j // per
            if blk % 2 == 1:
                qs.append(q3[j] * jnp.exp(b3[j] - last[blk * per - 1]))
                ks.append(zero)
            else:
                ks.append(k3[j] * jnp.exp(last[blk * per + per - 1] - b3[j]))
        lvl_q[w] = jnp.concatenate(qs, axis=0)
        lvl_k[w] = jnp.concatenate(ks, axis=0)
        yield
        w *= 2

    fg3 = fg.reshape(nsub, SUB, GW)
    z = k3
    band = [q * kk]
    for d in range(1, SUB):
        z = fg3 * pltpu.roll(z, 1, 1)
        band.append((q3 * z).reshape(CHUNK, GW))
        yield

    qb = q * jnp.exp(b)
    kb = kk * jnp.exp(blast - b)
    eb = jnp.exp(blast)
    yield
    scs = []
    for h in range(HEADS):
        hs = slice(h * HD, (h + 1) * HD)
        sc = 0.0
        for d in range(SUB):
            sc = jnp.where(m["band"][d], jnp.sum(band[d][:, hs], axis=1, keepdims=True), sc)
        for w in lvl_q:
            pw = _mm_nt(lvl_q[w][:, hs], lvl_k[w][:, hs])
            sc = _with_odd_rows(sc, jnp.where(_odd_rows(m["lvl"][w], w), pw, _odd_rows(sc, w)), w)
        scs.append(_mm(sc, v[:, hs]))
        yield
    hsl = [slice(h * HD, (h + 1) * HD) for h in range(HEADS)]
    inc = [_mm_tn(v[:, hs], kb[:, hs]) for hs in hsl]
    while states[c] is None:
        yield
    st = states[c]
    states[c + 1] = [st[h] * eb[:, hsl[h]] + inc[h] for h in range(HEADS)]
    yield
    for h in range(HEADS):
        o = scs[h] + _mm_nt(qb[:, hsl[h]], st[h])
        yield
        store(h, _rms(o, norm_w) * jax.nn.silu(p_ref[3, rows, hsl[h]]))
        yield


def _gdn_chunk(p_ref, r0, rows, ps, pst, cw, arow, dtrow, acol, dtcol, states, c, norm_w, store, m):
    hr = range(HEADS)
    hsl = [slice(h * HD, (h + 1) * HD) for h in hr]
    conv = []
    for i in range(3):
        win = p_ref[4 + i, pl.ds(r0, CHUNK + SUB), :]
        acc = win[SUB:SUB + CHUNK, :] * cw[CONV_W - 1:CONV_W, i * GW:(i + 1) * GW]
        for j in range(1, CONV_W):
            acc = acc + win[SUB - j:SUB - j + CHUNK, :] * cw[CONV_W - 1 - j:CONV_W - j, i * GW:(i + 1) * GW]
        conv.append(jax.nn.silu(acc))
        if i == 1:
            qn = [_l2n(conv[0][:, hs]) * (HD ** -0.5) for hs in hsl]
            kn = [_l2n(conv[1][:, hs]) for hs in hsl]
            beta = jax.nn.sigmoid(ps)
            gcol = _cumsum_rows(m["tril"], -jnp.exp(arow) * jax.nn.softplus(ps + dtrow))
            grow = _cumsum_lanes(-jnp.exp(acol) * jax.nn.softplus(pst + dtcol), m["triu"])
            bc = [beta[:, h:h + 1] for h in hr]
            gc = [gcol[:, HEADS + h:HEADS + h + 1] for h in hr]
            lmat = [jnp.where(m["causal"], jnp.exp(gc[h] - grow[HEADS + h:HEADS + h + 1, :]), 0.0) for h in hr]
            kbeta = [kn[h] * bc[h] for h in hr]
            ns = [jnp.where(m["strict"], _mm_nt(kbeta[h], kn[h]) * lmat[h], 0.0) for h in hr]
            tinv_stages = _tri_inv(ns, m)
        yield
    cv = conv[2]
    attn = [jnp.where(m["causal"], _mm_nt(qn[h], kn[h]) * lmat[h], 0.0) for h in hr]
    tinv = yield from tinv_stages
    eg = [jnp.exp(g) for g in gc]
    sol = [_mm(tinv[h], jnp.concatenate([cv[:, hsl[h]] * bc[h], kbeta[h] * eg[h]], axis=1)) for h in hr]
    yield
    kdec = [kn[h] * jnp.exp(gc[h][CHUNK - 1:CHUNK, :] - gc[h]) for h in hr]
    qdec = [qn[h] * eg[h] for h in hr]
    while states[c] is None:
        yield
    s = states[c]
    u = [sol[h][:, :HD] - _mm(sol[h][:, HD:], s[h]) for h in hr]
    yield
    states[c + 1] = [jnp.exp(gc[h][CHUNK - 1:CHUNK, :]) * s[h] + _mm_tn(kdec[h], u[h]) for h in hr]
    yield
    outs = [_mm(qdec[h], s[h]) + _mm(attn[h], u[h]) for h in hr]
    yield
    for h in hr:
        store(h, _rms(outs[h], norm_w) * jax.nn.silu(p_ref[7, rows, hsl[h]]))
        yield


def _gate_norm(o, gate, w):
    parts = []
    for h in range(HEADS):
        hs = slice(h * HD, (h + 1) * HD)
        parts.append(_rms(o[:, hs], w) * jax.nn.silu(gate[:, hs]))
    return jnp.concatenate(parts, axis=1)


def _prompt_body(x_ref, normw_ref, wmain_ref, wsmall_ref, wsmallt_ref, lbl_ref, convw_ref,
                 arow_ref, dtrow_ref, acol_ref, dtcol_ref, hgn_ref, gdnn_ref, fnorm_ref, wout_ref,
                 y_ref, shg_ref, sgdn_ref, sconv_ref,
                 p_ref, ps_ref, hbf_ref, obuf_ref, st_ref, sg_ref):
    ti = pl.program_id(1)
    tb = x_ref.shape[0]

    @pl.when(ti == 0)
    def _():
        st_ref[...] = jnp.zeros_like(st_ref)
        sg_ref[...] = jnp.zeros_like(sg_ref)
        p_ref[4:7, 0:SUB, :] = jnp.zeros((3, SUB, GW), f32)

    x = x_ref[...]
    hb = _rms(x, normw_ref[...]).astype(bf16)
    hbf_ref[...] = hb
    ps_ref[...] = jnp.dot(hb, wsmall_ref[...], preferred_element_type=f32)
    nchunk = tb // CHUNK

    def project():
        done = 0
        for j in range(nchunk):
            hj = hbf_ref[j * CHUNK:(j + 1) * CHUNK, :]
            for g in range(N_GROUPS):
                p_ref[g, SUB + j * CHUNK:SUB + (j + 1) * CHUNK, :] = jnp.dot(
                    hj, wmain_ref[:, g * GW:(g + 1) * GW], preferred_element_type=f32)
                done += 1
                if done % PROJ_RATE == 0:
                    yield

    m = _chunk_masks()
    lbv = _lower_bound(lbl_ref[...])
    cw = convw_ref[...]
    arow, dtrow, acol, dtcol = arow_ref[...], dtrow_ref[...], acol_ref[...], dtcol_ref[...]
    hgn, gdnn = hgn_ref[...], gdnn_ref[...]
    wsmallt = wsmallt_ref[...]

    hg_states = [[st_ref[h] for h in range(HEADS)]] + [None] * nchunk
    gdn_states = [[sg_ref[h] for h in range(HEADS)]] + [None] * nchunk

    def store_at(orows, col0):
        def store(h, val):
            obuf_ref[orows, col0 + h * HD:col0 + (h + 1) * HD] = val.astype(bf16)
        return store

    stages, starts = [project()], [0]
    for j in range(nchunk):
        ready = -(-N_GROUPS * (j + 1) // PROJ_RATE)
        starts += [ready, ready + HGRN_LAG]
        r0 = j * CHUNK
        rows = pl.ds(r0 + SUB, CHUNK)
        orows = pl.ds(r0, CHUNK)
        pst = _mm_nt(wsmallt, hbf_ref[orows, :])[0:2 * HEADS, :]
        stages.append(_gdn_chunk(p_ref, r0, rows, ps_ref[orows, :], pst, cw, arow, dtrow, acol, dtcol,
                                 gdn_states, j, gdnn, store_at(orows, GW), m))
        stages.append(_hgrn_chunk(p_ref, rows, lbv, hg_states, j, hgn, store_at(orows, 0), m))
    _interleave(stages, starts)
    for h in range(HEADS):
        st_ref[h] = hg_states[nchunk][h]
        sg_ref[h] = gdn_states[nchunk][h]

    p_ref[4:7, 0:SUB, :] = p_ref[4:7, tb:tb + SUB, :]

    out = jnp.dot(obuf_ref[...], wout_ref[...], preferred_element_type=f32)
    y_ref[...] = _rms(x + out, fnorm_ref[...])

    @pl.when(ti == pl.num_programs(1) - 1)
    def _():
        for h in range(HEADS):
            shg_ref[h] = st_ref[h].T
            sgdn_ref[h] = sg_ref[h]
        for i in range(3):
            tail = p_ref[4 + i, tb:tb + SUB, :]
            sconv_ref[:, i * GW:(i + 1) * GW] = tail[SUB - (CONV_W - 1):SUB, :]


def _const_spec(shape):
    nd = len(shape)
    return pl.BlockSpec(shape, lambda *_: (0,) * nd, pipeline_mode=pl.Buffered(1))


def _prompt_call(x, params):
    bsz, seq, d = x.shape
    tb = min(TOK_BLOCK, seq)
    assert seq % tb == 0 and tb % CHUNK == 0
    names = ("normw", "wmain", "wsmall", "wsmallt", "lbl", "convw", "arow", "dtrow", "acol", "dtcol",
             "hgn", "gdnn", "fnorm", "wout")
    consts = [params[k] for k in names]
    out_shape = (
        jax.ShapeDtypeStruct((bsz, seq, d), f32),
        jax.ShapeDtypeStruct((bsz, HEADS, HD, HD), f32),
        jax.ShapeDtypeStruct((bsz, HEADS, HD, HD), f32),
        jax.ShapeDtypeStruct((bsz, CONV_W - 1, 3 * GW), f32),
    )
    state_spec = pl.BlockSpec((None, HEADS, HD, HD), lambda b, t: (b, 0, 0, 0))
    return pl.pallas_call(
        _prompt_body,
        grid=(bsz, seq // tb),
        in_specs=[pl.BlockSpec((None, tb, d), lambda b, t: (b, t, 0))] + [_const_spec(c.shape) for c in consts],
        out_specs=(
            pl.BlockSpec((None, tb, d), lambda b, t: (b, t, 0)),
            state_spec, state_spec,
            pl.BlockSpec((None, CONV_W - 1, 3 * GW), lambda b, t: (b, 0, 0)),
        ),
        out_shape=out_shape,
        scratch_shapes=[
            pltpu.VMEM((N_GROUPS, tb + SUB, GW), f32),
            pltpu.VMEM((tb, 128), f32),
            pltpu.VMEM((tb, d), bf16),
            pltpu.VMEM((tb, 2 * GW), bf16),
            pltpu.VMEM((HEADS, HD, HD), f32),
            pltpu.VMEM((HEADS, HD, HD), f32),
        ],
        compiler_params=pltpu.CompilerParams(
            dimension_semantics=("arbitrary", "arbitrary"), vmem_limit_bytes=VMEM_LIMIT),
        name="prompt_mixer",
    )(x, *consts)


def _decode_body(x_ref, normw_ref, wmain_ref, wsmall_ref, lbl_ref, convw_ref, arow_ref, dtrow_ref,
                 hgn_ref, gdnn_ref, fnorm_ref, wout_ref, shg_in, sgdn_in, sconv_in,
                 y_ref, shg_out, sgdn_out, sconv_out,
                 p_ref, ps_ref, o_ref):
    i = pl.program_id(0)
    nb = shg_in.shape[0]

    @pl.when(i == 0)
    def _():
        hb = _rms(x_ref[...], normw_ref[...]).astype(bf16)
        for g in range(N_GROUPS):
            p_ref[g] = jnp.dot(hb, wmain_ref[:, g * GW:(g + 1) * GW], preferred_element_type=f32)
        ps_ref[...] = jnp.dot(hb, wsmall_ref[...], preferred_element_type=f32)

    r0 = pl.multiple_of(i * nb, nb)
    rows = pl.ds(r0, nb)
    lbv = _lower_bound(lbl_ref[...])
    q_hg = p_ref[0, rows, :]
    fg = lbv + (1.0 - lbv) * jax.nn.sigmoid(p_ref[1, rows, :])
    v_hg = p_ref[2, rows, :]

    cw = convw_ref[...]
    cc = 3 * GW
    new = jnp.concatenate([p_ref[4, rows, :], p_ref[5, rows, :], p_ref[6, rows, :]], axis=1)
    acc = new * cw[CONV_W - 1:CONV_W, :]
    for j in range(CONV_W - 1):
        acc = acc + sconv_in[:, j * cc:(j + 1) * cc] * cw[j:j + 1, :]
    for j in range(CONV_W - 2):
        sconv_out[:, j * cc:(j + 1) * cc] = sconv_in[:, (j + 1) * cc:(j + 2) * cc]
    sconv_out[:, (CONV_W - 2) * cc:(CONV_W - 1) * cc] = new
    qkv = jax.nn.silu(acc)
    ps = ps_ref[rows, :]
    beta = jax.nn.sigmoid(ps)
    eg = jnp.exp(-jnp.exp(arow_ref[...]) * jax.nn.softplus(ps + dtrow_ref[...]))

    pairs = [(h, j) for h in range(HEADS) for j in range(nb)]
    hsl = [slice(h * HD, (h + 1) * HD) for h in range(HEADS)]
    ft = [fg[:, hs].T for hs in hsl]
    q_h = [q_hg[:, hs].astype(bf16) for hs in hsl]
    kn = [_l2n(qkv[:, GW + h * HD:GW + (h + 1) * HD]) for h in range(HEADS)]
    knt = [k.T for k in kn]
    kn_b = [k.astype(bf16) for k in kn]
    qn = [(_l2n(qkv[:, hs]) * (HD ** -0.5)).astype(bf16) for hs in hsl]
    for h, j in pairs:
        fcol = ft[h][:, j:j + 1]
        s = fcol * shg_in[j, h] + (1.0 - fcol) * v_hg[j:j + 1, hsl[h]]
        shg_out[j, h] = s
        o_ref[i, j:j + 1, hsl[h]] = _mm(q_h[h], s)[j:j + 1, :]
    ks = {}
    for h, j in pairs:
        s = eg[j:j + 1, HEADS + h:HEADS + h + 1] * sgdn_in[j, h]
        sgdn_out[j, h] = s
        ks[h, j] = _mm(kn_b[h], s)[j:j + 1, :]
    for h, j in pairs:
        v_row = qkv[j:j + 1, 2 * GW + h * HD:2 * GW + (h + 1) * HD]
        delta = (v_row - ks[h, j]) * beta[j:j + 1, h:h + 1]
        s = sgdn_out[j, h] + knt[h][:, j:j + 1] * delta
        sgdn_out[j, h] = s
        o_ref[i, j:j + 1, GW + h * HD:GW + (h + 1) * HD] = _mm(qn[h], s)[j:j + 1, :]

    @pl.when(i == pl.num_programs(0) - 1)
    def _():
        o = o_ref[...].reshape(x_ref.shape[0], 2 * GW)
        og = jnp.concatenate([_gate_norm(o[:, :GW], p_ref[3], hgn_ref[...]),
                              _gate_norm(o[:, GW:], p_ref[7], gdnn_ref[...])], axis=1)
        out = jnp.dot(og.astype(bf16), wout_ref[...], preferred_element_type=f32)
        y_ref[...] = _rms(x_ref[...] + out, fnorm_ref[...])


def _decode_call(x, shg, sgdn, sconv, params):
    n, d = x.shape
    nb = min(DEC_BLOCK, n)
    assert n % nb == 0
    names = ("normw", "wmain", "wsmall", "lbl", "convw", "arow", "dtrow", "hgn", "gdnn", "fnorm", "wout")
    consts = [params[k] for k in names]
    state_spec = pl.BlockSpec((nb, HEADS, HD, HD), lambda i: (i, 0, 0, 0))
    conv_spec = pl.BlockSpec((nb, sconv.shape[1]), lambda i: (i, 0))
    return pl.pallas_call(
        _decode_body,
        grid=(n // nb,),
        in_specs=[_const_spec(x.shape)] + [_const_spec(c.shape) for c in consts]
        + [state_spec, state_spec, conv_spec],
        out_specs=(pl.BlockSpec(x.shape, lambda i: (0, 0)), state_spec, state_spec, conv_spec),
        out_shape=(
            jax.ShapeDtypeStruct((n, d), f32),
            jax.ShapeDtypeStruct(shg.shape, f32),
            jax.ShapeDtypeStruct(sgdn.shape, f32),
            jax.ShapeDtypeStruct(sconv.shape, f32),
        ),
        scratch_shapes=[
            pltpu.VMEM((N_GROUPS, n, GW), f32),
            pltpu.VMEM((n, 128), f32),
            pltpu.VMEM((n // nb, nb, 2 * GW), f32),
        ],
        compiler_params=pltpu.CompilerParams(
            dimension_semantics=("arbitrary",), vmem_limit_bytes=VMEM_LIMIT),
        name="decode_step",
    )(x, *consts, shg, sgdn, sconv)


def _prepare(norm_w, w_in, hg_lb_logits, conv_w, gdn_a_log, gdn_dt_bias, hg_out_norm, gdn_out_norm,
             w_out, final_norm):
    d = w_in.shape[1]
    w = w_in[0]
    nmain = N_GROUPS * GW
    small = w[:, nmain:]
    zrow = jnp.zeros((1, HEADS), f32)
    a = gdn_a_log[0].reshape(1, HEADS).astype(f32)
    dt = gdn_dt_bias[0].reshape(1, HEADS).astype(f32)
    return dict(
        normw=norm_w[0].reshape(1, d),
        wmain=w.astype(bf16),
        wsmall=jnp.pad(small, ((0, 0), (0, 128 - 2 * HEADS))).astype(bf16),
        wsmallt=jnp.pad(small.T, ((0, 16 - 2 * HEADS), (0, 0))).astype(bf16),
        lbl=hg_lb_logits.astype(f32),
        convw=conv_w[0].astype(f32),
        arow=jnp.pad(jnp.concatenate([zrow, a], axis=1), ((0, 0), (0, 128 - 2 * HEADS))),
        dtrow=jnp.pad(jnp.concatenate([zrow, dt], axis=1), ((0, 0), (0, 128 - 2 * HEADS))),
        acol=jnp.concatenate([zrow, a], axis=1).reshape(2 * HEADS, 1),
        dtcol=jnp.concatenate([zrow, dt], axis=1).reshape(2 * HEADS, 1),
        hgn=hg_out_norm[0].reshape(1, HD),
        gdnn=gdn_out_norm[0].reshape(1, HD),
        fnorm=final_norm.reshape(1, d),
        wout=w_out[0].astype(bf16),
    )


def kernel(x_prompt, x_sample, state_hgrn, state_gdn, state_gdn_conv, norm_w, w_in, hg_lb_logits, conv_w, gdn_a_log, gdn_dt_bias, hg_out_norm, gdn_out_norm, w_out, final_norm):
    assert w_in.shape[0] == 1, "single-layer kernel"
    params = _prepare(norm_w, w_in, hg_lb_logits, conv_w, gdn_a_log, gdn_dt_bias, hg_out_norm,
                      gdn_out_norm, w_out, final_norm)
    y_p, hg_p, gdn_p, conv_p = _prompt_call(x_prompt, params)
    n, dec_seq, d = x_sample.shape
    assert dec_seq == 1, "decode path is a single-token step"
    conv_shape = state_gdn_conv.shape[1:]
    y_s, hg_s, gdn_s, conv_s = _decode_call(x_sample.reshape(n, d), state_hgrn[0], state_gdn[0],
                                            state_gdn_conv[0].reshape(n, -1), params)
    return (y_p, y_s.reshape(n, 1, d), hg_p[None], gdn_p[None], conv_p[None],
            hg_s[None], gdn_s[None], conv_s.reshape(conv_shape)[None])
```

```python
import jax
import jax.numpy as jnp
from jax import lax
from jax.experimental import pallas as pl
from jax.experimental.pallas import tpu as pltpu

f32 = jnp.float32
bf16 = jnp.bfloat16

HEADS = 4
HD = 128
GW = HEADS * HD
N_GROUPS = 8
CONV_W = 4
CHUNK = 128
SUB = 8
EPS = 1e-6
TOK_BLOCK = 512
DEC_BLOCK = 16
PROJ_RATE = 4
HGRN_LAG = 1
VMEM_LIMIT = 56 * 1024 * 1024


def _mm(a, b):
    return jnp.dot(a.astype(bf16), b.astype(bf16), preferred_element_type=f32)


def _mm_nt(a, b):
    return lax.dot_general(a.astype(bf16), b.astype(bf16), (((1,), (1,)), ((), ())),
                           preferred_element_type=f32)


def _mm_tn(a, b):
    return lax.dot_general(a.astype(bf16), b.astype(bf16), (((0,), (0,)), ((), ())),
                           preferred_element_type=f32)


def _split3(x):
    a1 = x.astype(bf16)
    r1 = x - a1.astype(f32)
    a2 = r1.astype(bf16)
    r2 = r1 - a2.astype(f32)
    return a1, a2, r2.astype(bf16)


def _cumsum_rows(tril, x):
    a1, a2, a3 = _split3(x)
    d = lambda a: jnp.dot(tril, a, preferred_element_type=f32)
    return d(a1) + d(a2) + d(a3)


def _cumsum_lanes(x, triu):
    a1, a2, a3 = _split3(x)
    d = lambda a: jnp.dot(a, triu, preferred_element_type=f32)
    return d(a1) + d(a2) + d(a3)


def _rms(x, w):
    return x * lax.rsqrt(jnp.mean(x * x, axis=-1, keepdims=True) + EPS) * w


def _l2n(x):
    return x * lax.rsqrt(jnp.sum(x * x, axis=-1, keepdims=True) + EPS)


def _lower_bound(lbl):
    m = jnp.max(lbl, axis=0, keepdims=True)
    e = jnp.exp(lbl - m)
    return e[0:1, :] / jnp.sum(e, axis=0, keepdims=True)


def _chunk_masks():
    t = lax.broadcasted_iota(jnp.int32, (CHUNK, CHUNK), 0)
    s = lax.broadcasted_iota(jnp.int32, (CHUNK, CHUNK), 1)
    m = {}
    m["causal"] = s <= t
    m["strict"] = s < t
    m["eye"] = jnp.where(s == t, 1.0, 0.0).astype(f32)
    m["tril"] = jnp.where(s <= t, 1.0, 0.0).astype(bf16)
    m["triu"] = jnp.where(t <= s, 1.0, 0.0).astype(bf16)
    m["blk"] = (t // SUB) == (s // SUB)
    m["band"] = [(s == t - d) & ((t % SUB) >= d) for d in range(SUB)]
    lv = {}
    w = SUB
    while w < CHUNK:
        lv[w] = ((t // (2 * w)) == (s // (2 * w))) & (((t // w) % 2) == 1) & (((s // w) % 2) == 0)
        w *= 2
    m["lvl"] = lv
    return m


def _odd_rows(x, w):
    return jnp.concatenate([x[i * w:(i + 1) * w] for i in range(1, CHUNK // w, 2)], axis=0)


def _with_odd_rows(x, odd, w):
    parts = []
    for i in range(CHUNK // w):
        parts.append(odd[(i // 2) * w:(i // 2 + 1) * w] if i % 2 else x[i * w:(i + 1) * w])
    return jnp.concatenate(parts, axis=0)


def _tri_inv(ns, m):
    eye = m["eye"]
    ps = [jnp.where(m["blk"], -n, 0.0) for n in ns]
    ts = [eye + p for p in ps]
    w = 2
    while w < SUB:
        ps = [_mm(p, p) for p in ps]
        yield
        ts = [_mm(t, eye + p) for t, p in zip(ts, ps)]
        yield
        w *= 2
    w = SUB
    while w < CHUNK:
        zero = jnp.zeros_like(ts[0])
        mo = _odd_rows(m["lvl"][w], w)
        tbs = [t.astype(bf16) for t in ts]
        fts = [_mm(jnp.where(mo, _odd_rows(n, w), 0.0), tb) for n, tb in zip(ns, tbs)]
        yield
        ts = [_with_odd_rows(t, _odd_rows(t, w) - _mm(_odd_rows(tb, w), _with_odd_rows(zero, ft, w)), w)
              for t, tb, ft in zip(ts, tbs, fts)]
        yield
        w *= 2
    return ts


def _interleave(stages, starts):
    pending = sorted(zip(starts, range(len(stages))))
    live, turn = [], 0
    while live or pending:
        while pending and pending[0][0] <= turn:
            live.append(stages[pending.pop(0)[1]])
        for g in list(live):
            try:
                next(g)
            except StopIteration:
                live.remove(g)
        turn += 1


def _hgrn_chunk(p_ref, rows, lbv, states, c, norm_w, store, m):
    q, hf, v = p_ref[0, rows, :], p_ref[1, rows, :], p_ref[2, rows, :]
    fg = lbv + (1.0 - lbv) * jax.nn.sigmoid(hf)
    kk = 1.0 - fg
    b = _cumsum_rows(m["tril"], jnp.log(fg))
    yield
    nsub = CHUNK // SUB
    b3 = b.reshape(nsub, SUB, GW)
    q3 = q.reshape(nsub, SUB, GW)
    k3 = kk.reshape(nsub, SUB, GW)
    last = [b3[j, SUB - 1:SUB, :] for j in range(nsub)]
    blast = last[nsub - 1]
    zero = jnp.zeros((SUB, GW), f32)

    lvl_q, lvl_k = {}, {}
    w = SUB
    while w < CHUNK:
        per = w // SUB
        qs, ks = [], []
        for j in range(nsub):
            blk = j // per
            if blk % 2 == 1:
                qs.append(q3[j] * jnp.exp(b3[j] - last[blk * per - 1]))
                ks.append(zero)
            else:
                ks.append(k3[j] * jnp.exp(last[blk * per + per - 1] - b3[j]))
        lvl_q[w] = jnp.concatenate(qs, axis=0)
        lvl_k[w] = jnp.concatenate(ks, axis=0)
        yield
        w *= 2

    fg3 = fg.reshape(nsub, SUB, GW)
    z = k3
    band = [q * kk]
    for d in range(1, SUB):
        z = fg3 * pltpu.roll(z, 1, 1)
        band.append((q3 * z).reshape(CHUNK, GW))
        yield

    qb = q * jnp.exp(b)
    kb = kk * jnp.exp(blast - b)
    eb = jnp.exp(blast)
    yield
    scs = []
    for h in range(HEADS):
        hs = slice(h * HD, (h + 1) * HD)
        sc = 0.0
        for d in range(SUB):
            sc = jnp.where(m["band"][d], jnp.sum(band[d][:, hs], axis=1, keepdims=True), sc)
        for w in lvl_q:
            pw = _mm_nt(lvl_q[w][:, hs], lvl_k[w][:, hs])
            sc = _with_odd_rows(sc, jnp.where(_odd_rows(m["lvl"][w], w), pw, _odd_rows(sc, w)), w)
        scs.append(_mm(sc, v[:, hs]))
        yield
    hsl = [slice(h * HD, (h + 1) * HD) for h in range(HEADS)]
    inc = [_mm_tn(v[:, hs], kb[:, hs]) for hs in hsl]
    while states[c] is None:
        yield
    st = states[c]
    states[c + 1] = [st[h] * eb[:, hsl[h]] + inc[h] for h in range(HEADS)]
    yield
    for h in range(HEADS):
        o = scs[h] + _mm_nt(qb[:, hsl[h]], st[h])
        yield
        store(h, _rms(o, norm_w) * jax.nn.silu(p_ref[3, rows, hsl[h]]))
        yield


def _gdn_chunk(p_ref, r0, rows, ps, pst, cw, arow, dtrow, acol, dtcol, states, c, norm_w, store, m):
    hr = range(HEADS)
    hsl = [slice(h * HD, (h + 1) * HD) for h in hr]
    conv = []
    for i in range(3):
        win = p_ref[4 + i, pl.ds(r0, CHUNK + SUB), :]
        acc = win[SUB:SUB + CHUNK, :] * cw[CONV_W - 1:CONV_W, i * GW:(i + 1) * GW]
        for j in range(1, CONV_W):
            acc = acc + win[SUB - j:SUB - j + CHUNK, :] * cw[CONV_W - 1 - j:CONV_W - j, i * GW:(i + 1) * GW]
        conv.append(jax.nn.silu(acc))
        if i == 1:
            qn = [_l2n(conv[0][:, hs]) * (HD ** -0.5) for hs in hsl]
            kn = [_l2n(conv[1][:, hs]) for hs in hsl]
            beta = jax.nn.sigmoid(ps)
            gcol = _cumsum_rows(m["tril"], -jnp.exp(arow) * jax.nn.softplus(ps + dtrow))
            grow = _cumsum_lanes(-jnp.exp(acol) * jax.nn.softplus(pst + dtcol), m["triu"])
            bc = [beta[:, h:h + 1] for h in hr]
            gc = [gcol[:, HEADS + h:HEADS + h + 1] for h in hr]
            lmat = [jnp.where(m["causal"], jnp.exp(gc[h] - grow[HEADS + h:HEADS + h + 1, :]), 0.0) for h in hr]
            kbeta = [kn[h] * bc[h] for h in hr]
            ns = [jnp.where(m["strict"], _mm_nt(kbeta[h], kn[h]) * lmat[h], 0.0) for h in hr]
            tinv_stages = _tri_inv(ns, m)
        yield
    cv = conv[2]
    attn = [jnp.where(m["causal"], _mm_nt(qn[h], kn[h]) * lmat[h], 0.0) for h in hr]
    tinv = yield from tinv_stages
    eg = [jnp.exp(g) for g in gc]
    sol = [_mm(tinv[h], jnp.concatenate([cv[:, hsl[h]] * bc[h], kbeta[h] * eg[h]], axis=1)) for h in hr]
    yield
    kdec = [kn[h] * jnp.exp(gc[h][CHUNK - 1:CHUNK, :] - gc[h]) for h in hr]
    qdec = [qn[h] * eg[h] for h in hr]
    while states[c] is None:
        yield
    s = states[c]
    u = [sol[h][:, :HD] - _mm(sol[h][:, HD:], s[h]) for h in hr]
    yield
    states[c + 1] = [jnp.exp(gc[h][CHUNK - 1:CHUNK, :]) * s[h] + _mm_tn(kdec[h], u[h]) for h in hr]
    yield
    outs = [_mm(qdec[h], s[h]) + _mm(attn[h], u[h]) for h in hr]
    yield
    for h in hr:
        store(h, _rms(outs[h], norm_w) * jax.nn.silu(p_ref[7, rows, hsl[h]]))
        yield


def _gate_norm(o, gate, w):
    parts = []
    for h in range(HEADS):
        hs = slice(h * HD, (h + 1) * HD)
        parts.append(_rms(o[:, hs], w) * jax.nn.silu(gate[:, hs]))
    return jnp.concatenate(parts, axis=1)


def _prompt_body(x_ref, normw_ref, wmain_ref, wsmall_ref, wsmallt_ref, lbl_ref, convw_ref,
                 arow_ref, dtrow_ref, acol_ref, dtcol_ref, hgn_ref, gdnn_ref, fnorm_ref, wout_ref,
                 y_ref, shg_ref, sgdn_ref, sconv_ref,
                 p_ref, ps_ref, hbf_ref, obuf_ref, st_ref, sg_ref):
    ti = pl.program_id(1)
    tb = x_ref.shape[0]

    @pl.when(ti == 0)
    def _():
        st_ref[...] = jnp.zeros_like(st_ref)
        sg_ref[...] = jnp.zeros_like(sg_ref)
        p_ref[4:7, 0:SUB, :] = jnp.zeros((3, SUB, GW), f32)

    x = x_ref[...]
    hb = _rms(x, normw_ref[...]).astype(bf16)
    hbf_ref[...] = hb
    ps_ref[...] = jnp.dot(hb, wsmall_ref[...], preferred_element_type=f32)
    nchunk = tb // CHUNK

    def project():
        done = 0
        for j in range(nchunk):
            hj = hbf_ref[j * CHUNK:(j + 1) * CHUNK, :]
            for g in range(N_GROUPS):
                p_ref[g, SUB + j * CHUNK:SUB + (j + 1) * CHUNK, :] = jnp.dot(
                    hj, wmain_ref[:, g * GW:(g + 1) * GW], preferred_element_type=f32)
                done += 1
                if done % PROJ_RATE == 0:
                    yield

    m = _chunk_masks()
    lbv = _lower_bound(lbl_ref[...])
    cw = convw_ref[...]
    arow, dtrow, acol, dtcol = arow_ref[...], dtrow_ref[...], acol_ref[...], dtcol_ref[...]
    hgn, gdnn = hgn_ref[...], gdnn_ref[...]
    wsmallt = wsmallt_ref[...]

    hg_states = [[st_ref[h] for h in range(HEADS)]] + [None] * nchunk
    gdn_states = [[sg_ref[h] for h in range(HEADS)]] + [None] * nchunk

    def store_at(orows, col0):
        def store(h, val):
            obuf_ref[orows, col0 + h * HD:col0 + (h + 1) * HD] = val.astype(bf16)
        return store

    stages, starts = [project()], [0]
    for j in range(nchunk):
        ready = -(-N_GROUPS * (j + 1) // PROJ_RATE)
        starts += [ready, ready + HGRN_LAG]
        r0 = j * CHUNK
        rows = pl.ds(r0 + SUB, CHUNK)
        orows = pl.ds(r0, CHUNK)
        pst = _mm_nt(wsmallt, hbf_ref[orows, :])[0:2 * HEADS, :]
        stages.append(_gdn_chunk(p_ref, r0, rows, ps_ref[orows, :], pst, cw, arow, dtrow, acol, dtcol,
                                 gdn_states, j, gdnn, store_at(orows, GW), m))
        stages.append(_hgrn_chunk(p_ref, rows, lbv, hg_states, j, hgn, store_at(orows, 0), m))
    _interleave(stages, starts)
    for h in range(HEADS):
        st_ref[h] = hg_states[nchunk][h]
        sg_ref[h] = gdn_states[nchunk][h]

    p_ref[4:7, 0:SUB, :] = p_ref[4:7, tb:tb + SUB, :]

    out = jnp.dot(obuf_ref[...], wout_ref[...], preferred_element_type=f32)
    y_ref[...] = _rms(x + out, fnorm_ref[...])

    @pl.when(ti == pl.num_programs(1) - 1)
    def _():
        for h in range(HEADS):
            shg_ref[h] = st_ref[h].T
            sgdn_ref[h] = sg_ref[h]
        for i in range(3):
            tail = p_ref[4 + i, tb:tb + SUB, :]
            sconv_ref[:, i * GW:(i + 1) * GW] = tail[SUB - (CONV_W - 1):SUB, :]


def _const_spec(shape, single_buffer=False):
    nd = len(shape)
    if single_buffer:
        return pl.BlockSpec(shape, lambda *_: (0,) * nd, pipeline_mode=pl.Buffered(1))
    return pl.BlockSpec(shape, lambda *_: (0,) * nd)


def _prompt_call(x, params):
    bsz, seq, d = x.shape
    tb = min(TOK_BLOCK, seq)
    assert seq % tb == 0 and tb % CHUNK == 0
    names = ("normw", "wmain", "wsmall", "wsmallt", "lbl", "convw", "arow", "dtrow", "acol", "dtcol",
             "hgn", "gdnn", "fnorm", "wout")
    consts = [params[k] for k in names]
    out_shape = (
        jax.ShapeDtypeStruct((bsz, seq, d), f32),
        jax.ShapeDtypeStruct((bsz, HEADS, HD, HD), f32),
        jax.ShapeDtypeStruct((bsz, HEADS, HD, HD), f32),
        jax.ShapeDtypeStruct((bsz, CONV_W - 1, 3 * GW), f32),
    )
    state_spec = pl.BlockSpec((None, HEADS, HD, HD), lambda b, t: (b, 0, 0, 0))
    return pl.pallas_call(
        _prompt_body,
        grid=(bsz, seq // tb),
        in_specs=[pl.BlockSpec((None, tb, d), lambda b, t: (b, t, 0))] + [_const_spec(c.shape) for c in consts],
        out_specs=(
            pl.BlockSpec((None, tb, d), lambda b, t: (b, t, 0)),
            state_spec, state_spec,
            pl.BlockSpec((None, CONV_W - 1, 3 * GW), lambda b, t: (b, 0, 0)),
        ),
        out_shape=out_shape,
        scratch_shapes=[
            pltpu.VMEM((N_GROUPS, tb + SUB, GW), f32),
            pltpu.VMEM((tb, 128), f32),
            pltpu.VMEM((tb, d), bf16),
            pltpu.VMEM((tb, 2 * GW), bf16),
            pltpu.VMEM((HEADS, HD, HD), f32),
            pltpu.VMEM((HEADS, HD, HD), f32),
        ],
        compiler_params=pltpu.CompilerParams(
            dimension_semantics=("arbitrary", "arbitrary"), vmem_limit_bytes=VMEM_LIMIT),
        name="prompt_mixer",
    )(x, *consts)


def _decode_body(x_ref, normw_ref, wmain_ref, wsmall_ref, lbl_ref, convw_ref, arow_ref, dtrow_ref,
                 hgn_ref, gdnn_ref, fnorm_ref, wout_ref, shg_in, sgdn_in, sconv_in,
                 y_ref, shg_out, sgdn_out, sconv_out,
                 p_ref, ps_ref, o_ref):
    i = pl.program_id(0)
    nb = shg_in.shape[0]

    @pl.when(i == 0)
    def _():
        hb = _rms(x_ref[...], normw_ref[...]).astype(bf16)
        for g in range(N_GROUPS):
            p_ref[g] = jnp.dot(hb, wmain_ref[:, g * GW:(g + 1) * GW], preferred_element_type=f32)
        ps_ref[...] = jnp.dot(hb, wsmall_ref[...], preferred_element_type=f32)

    r0 = pl.multiple_of(i * nb, nb)
    rows = pl.ds(r0, nb)
    lbv = _lower_bound(lbl_ref[...])
    q_hg = p_ref[0, rows, :]
    fg = lbv + (1.0 - lbv) * jax.nn.sigmoid(p_ref[1, rows, :])
    v_hg = p_ref[2, rows, :]

    cw = convw_ref[...]
    cc = 3 * GW
    new = jnp.concatenate([p_ref[4, rows, :], p_ref[5, rows, :], p_ref[6, rows, :]], axis=1)
    acc = new * cw[CONV_W - 1:CONV_W, :]
    for j in range(CONV_W - 1):
        acc = acc + sconv_in[:, j * cc:(j + 1) * cc] * cw[j:j + 1, :]
    for j in range(CONV_W - 2):
        sconv_out[:, j * cc:(j + 1) * cc] = sconv_in[:, (j + 1) * cc:(j + 2) * cc]
    sconv_out[:, (CONV_W - 2) * cc:(CONV_W - 1) * cc] = new
    qkv = jax.nn.silu(acc)
    ps = ps_ref[rows, :]
    beta = jax.nn.sigmoid(ps)
    eg = jnp.exp(-jnp.exp(arow_ref[...]) * jax.nn.softplus(ps + dtrow_ref[...]))

    pairs = [(h, j) for h in range(HEADS) for j in range(nb)]
    hsl = [slice(h * HD, (h + 1) * HD) for h in range(HEADS)]
    ft = [fg[:, hs].T for hs in hsl]
    q_h = [q_hg[:, hs].astype(bf16) for hs in hsl]
    kn = [_l2n(qkv[:, GW + h * HD:GW + (h + 1) * HD]) for h in range(HEADS)]
    knt = [k.T for k in kn]
    kn_b = [k.astype(bf16) for k in kn]
    qn = [(_l2n(qkv[:, hs]) * (HD ** -0.5)).astype(bf16) for hs in hsl]
    for h, j in pairs:
        fcol = ft[h][:, j:j + 1]
        s = fcol * shg_in[j, h] + (1.0 - fcol) * v_hg[j:j + 1, hsl[h]]
        shg_out[j, h] = s
        o_ref[i, j:j + 1, hsl[h]] = _mm(q_h[h], s)[j:j + 1, :]
    ks = {}
    for h, j in pairs:
        s = eg[j:j + 1, HEADS + h:HEADS + h + 1] * sgdn_in[j, h]
        sgdn_out[j, h] = s
        ks[h, j] = _mm(kn_b[h], s)[j:j + 1, :]
    for h, j in pairs:
        v_row = qkv[j:j + 1, 2 * GW + h * HD:2 * GW + (h + 1) * HD]
        delta = (v_row - ks[h, j]) * beta[j:j + 1, h:h + 1]
        s = sgdn_out[j, h] + knt[h][:, j:j + 1] * delta
        sgdn_out[j, h] = s
        o_ref[i, j:j + 1, GW + h * HD:GW + (h + 1) * HD] = _mm(qn[h], s)[j:j + 1, :]

    @pl.when(i == pl.num_programs(0) - 1)
    def _():
        o = o_ref[...].reshape(x_ref.shape[0], 2 * GW)
        og = jnp.concatenate([_gate_norm(o[:, :GW], p_ref[3], hgn_ref[...]),
                              _gate_norm(o[:, GW:], p_ref[7], gdnn_ref[...])], axis=1)
        out = jnp.dot(og.astype(bf16), wout_ref[...], preferred_element_type=f32)
        y_ref[...] = _rms(x_ref[...] + out, fnorm_ref[...])


def _decode_call(x, shg, sgdn, sconv, params):
    n, d = x.shape
    nb = min(DEC_BLOCK, n)
    assert n % nb == 0
    names = ("normw", "wmain", "wsmall", "lbl", "convw", "arow", "dtrow", "hgn", "gdnn", "fnorm", "wout")
    consts = [params[k] for k in names]
    state_spec = pl.BlockSpec((nb, HEADS, HD, HD), lambda i: (i, 0, 0, 0))
    conv_spec = pl.BlockSpec((nb, sconv.shape[1]), lambda i: (i, 0))
    return pl.pallas_call(
        _decode_body,
        grid=(n // nb,),
        in_specs=[_const_spec(x.shape, True)] + [_const_spec(c.shape, True) for c in consts]
        + [state_spec, state_spec, conv_spec],
        out_specs=(pl.BlockSpec(x.shape, lambda i: (0, 0)), state_spec, state_spec, conv_spec),
        out_shape=(
            jax.ShapeDtypeStruct((n, d), f32),
            jax.ShapeDtypeStruct(shg.shape, f32),
            jax.ShapeDtypeStruct(sgdn.shape, f32),
            jax.ShapeDtypeStruct(sconv.shape, f32),
        ),
        scratch_shapes=[
            pltpu.VMEM((N_GROUPS, n, GW), f32),
            pltpu.VMEM((n, 128), f32),
            pltpu.VMEM((n // nb, nb, 2 * GW), f32),
        ],
        compiler_params=pltpu.CompilerParams(
            dimension_semantics=("arbitrary",), vmem_limit_bytes=VMEM_LIMIT),
        name="decode_step",
    )(x, *consts, shg, sgdn, sconv)


def _prepare(norm_w, w_in, hg_lb_logits, conv_w, gdn_a_log, gdn_dt_bias, hg_out_norm, gdn_out_norm,
             w_out, final_norm):
    d = w_in.shape[1]
    w = w_in[0]
    nmain = N_GROUPS * GW
    small = w[:, nmain:]
    zrow = jnp.zeros((1, HEADS), f32)
    a = gdn_a_log[0].reshape(1, HEADS).astype(f32)
    dt = gdn_dt_bias[0].reshape(1, HEADS).astype(f32)
    return dict(
        normw=norm_w[0].reshape(1, d),
        wmain=w.astype(bf16),
        wsmall=jnp.pad(small, ((0, 0), (0, 128 - 2 * HEADS))).astype(bf16),
        wsmallt=jnp.pad(small.T, ((0, 16 - 2 * HEADS), (0, 0))).astype(bf16),
        lbl=hg_lb_logits.astype(f32),
        convw=conv_w[0].astype(f32),
        arow=jnp.pad(jnp.concatenate([zrow, a], axis=1), ((0, 0), (0, 128 - 2 * HEADS))),
        dtrow=jnp.pad(jnp.concatenate([zrow, dt], axis=1), ((0, 0), (0, 128 - 2 * HEADS))),
        acol=jnp.concatenate([zrow, a], axis=1).reshape(2 * HEADS, 1),
        dtcol=jnp.concatenate([zrow, dt], axis=1).reshape(2 * HEADS, 1),
        hgn=hg_out_norm[0].reshape(1, HD),
        gdnn=gdn_out_norm[0].reshape(1, HD),
        fnorm=final_norm.reshape(1, d),
        wout=w_out[0].astype(bf16),
    )


def kernel(x_prompt, x_sample, state_hgrn, state_gdn, state_gdn_conv, norm_w, w_in, hg_lb_logits, conv_w, gdn_a_log, gdn_dt_bias, hg_out_norm, gdn_out_norm, w_out, final_norm):
    assert w_in.shape[0] == 1, "single-layer kernel"
    params = _prepare(norm_w, w_in, hg_lb_logits, conv_w, gdn_a_log, gdn_dt_bias, hg_out_norm,
                      gdn_out_norm, w_out, final_norm)
    y_p, hg_p, gdn_p, conv_p = _prompt_call(x_prompt, params)
    n, dec_seq, d = x_sample.shape
    assert dec_seq == 1, "decode path is a single-token step"
    conv_shape = state_gdn_conv.shape[1:]
    y_s, hg_s, gdn_s, conv_s = _decode_call(x_sample.reshape(n, d), state_hgrn[0], state_gdn[0],
                                            state_gdn_conv[0].reshape(n, -1), params)
    return (y_p, y_s.reshape(n, 1, d), hg_p[None], gdn_p[None], conv_p[None],
            hg_s[None], gdn_s[None], conv_s.reshape(conv_shape)[None])
```
